```python
import jax, jax.numpy as jnp
from jax import lax
import numpy as np

D_MODEL = 2048
BATCH = 1
SEQ = 8192
DEPTH = 1
DEC_BATCH = 32
DEC_SEQ = 8
PAST_LEN = 16384
PAGE_SIZE = 128

PLE_DIM = 256
D_FF = 5632
EPS = 1e-6
A_HEADS = 16
A_HEAD_DIM = 128
A_WIDTH = A_HEADS * A_HEAD_DIM
MOBA_BLOCK = 256
MOBA_TOPK = 3
MOBA_QCHUNK = 32
B_HEADS = 4
B_KEY_DIM = 256
B_VAL_DIM = 512
BK_WIDTH = B_HEADS * B_KEY_DIM
BV_WIDTH = B_HEADS * B_VAL_DIM
GATE_RANK = 16
GATE_TEMP = 16.0
GLA_CHUNK = 64
IN_SPLITS = (A_WIDTH, A_WIDTH, A_WIDTH, BK_WIDTH, BK_WIDTH, BV_WIDTH, BV_WIDTH, GATE_RANK, D_MODEL, D_MODEL)

kernel_name = 'hybrid_moba_gla_macaron_step'


def _split_points():
    return [int(c) for c in np.cumsum(IN_SPLITS)[:-1]]


def rmsnorm(x, g):
    xf = x.astype(jnp.float32)
    y = xf * lax.rsqrt(jnp.mean(xf * xf, axis=-1, keepdims=True) + EPS)
    return (y * g.astype(jnp.float32)).astype(x.dtype)


def swiglu(x, wg, wu, wd):
    return (jax.nn.silu(x @ wg) * (x @ wu)) @ wd


def alibi_slopes():
    return jnp.exp2(-8.0 * jnp.arange(1, A_HEADS + 1, dtype=jnp.float32) / A_HEADS)


def moba_select(q, kmean, qpos):
    nb = kmean.shape[0]
    own = qpos // MOBA_BLOCK
    s = jnp.einsum('thd,nhd->thn', q.astype(jnp.float32), kmean)
    fully_past = jnp.arange(nb)[None, None, :] < own[:, None, None]
    s = jnp.where(fully_past, s, -jnp.inf)
    _, top = lax.top_k(s, MOBA_TOPK)
    own_b = jnp.broadcast_to(own[:, None, None], top.shape[:2] + (1,))
    blocks = jnp.concatenate([top.astype(jnp.int32), own_b.astype(jnp.int32)], axis=-1)
    ok = jnp.concatenate([top < own[:, None, None], jnp.ones(own_b.shape, dtype=bool)], axis=-1)
    return blocks, ok


def moba_attend(q, kg, vg, qpos, kpos, ok):
    t, h, dh = q.shape
    logits = jnp.einsum('thd,thnrd->thnr', q, kg, preferred_element_type=jnp.float32) * (dh ** -0.5)
    dist = (qpos[:, None, None, None] - kpos).astype(jnp.float32)
    logits = logits - alibi_slopes()[None, :, None, None] * dist
    mask = ok[..., None] & (kpos <= qpos[:, None, None, None])
    logits = jnp.where(mask, logits, -jnp.inf)
    p = jax.nn.softmax(logits.reshape(t, h, -1), axis=-1).reshape(logits.shape)
    o = jnp.einsum('thnr,thnrd->thd', p.astype(vg.dtype), vg, preferred_element_type=jnp.float32)
    return o.astype(q.dtype)


def moba_prompt_seq(q, k, v):
    s_len, h, dh = q.shape
    nb = max(-(-s_len // MOBA_BLOCK), MOBA_TOPK)
    pad = nb * MOBA_BLOCK - s_len
    kb = jnp.pad(k, ((0, pad), (0, 0), (0, 0))).reshape(nb, MOBA_BLOCK, h, dh)
    vb = jnp.pad(v, ((0, pad), (0, 0), (0, 0))).reshape(nb, MOBA_BLOCK, h, dh)
    kmean = jnp.sum(kb, axis=1, dtype=jnp.float32) / MOBA_BLOCK
    kb = kb.transpose(0, 2, 1, 3)
    vb = vb.transpose(0, 2, 1, 3)
    hidx = jnp.arange(h)[None, :, None]
    offs = jnp.arange(MOBA_BLOCK, dtype=jnp.int32)
    nq = s_len // MOBA_QCHUNK

    def chunk(args):
        qc, start = args
        qpos = start + jnp.arange(MOBA_QCHUNK, dtype=jnp.int32)
        blocks, ok = moba_select(qc, kmean, qpos)
        kg = kb[blocks, hidx]
        vg = vb[blocks, hidx]
        kpos = blocks[..., None] * MOBA_BLOCK + offs
        return moba_attend(qc, kg, vg, qpos, kpos, ok)

    starts = jnp.arange(nq, dtype=jnp.int32) * MOBA_QCHUNK
    o = lax.map(chunk, (q.reshape(nq, MOBA_QCHUNK, h, dh), starts))
    return o.reshape(s_len, h, dh)


def moba_sample_seq(q, k_new, v_new, pt, cache_k, cache_v, li):
    t, h, dh = q.shape
    n_pages = pt.shape[0]
    page = cache_k.shape[2]
    past = n_pages * page
    nb = max(-(-(past + t) // MOBA_BLOCK), MOBA_TOPK)
    page_sums = jnp.sum(cache_k[li, pt], axis=1, dtype=jnp.float32)
    page_blk = jnp.arange(n_pages, dtype=jnp.int32) * page // MOBA_BLOCK
    qpos = past + jnp.arange(t, dtype=jnp.int32)
    ksum = (jax.ops.segment_sum(page_sums, page_blk, num_segments=nb)
            + jax.ops.segment_sum(k_new.astype(jnp.float32), qpos // MOBA_BLOCK, num_segments=nb))
    blocks, ok = moba_select(q, ksum / MOBA_BLOCK, qpos)
    kpos = blocks[..., None] * MOBA_BLOCK + jnp.arange(MOBA_BLOCK, dtype=jnp.int32)
    in_past = (kpos < past)[..., None]
    pc = jnp.minimum(kpos, past - 1)
    phys = pt[pc // page]
    row = pc % page
    pn = jnp.clip(kpos - past, 0, t - 1)
    hidx = jnp.arange(h)[None, :, None, None]
    kg = jnp.where(in_past, cache_k[li, phys, row, hidx], k_new[pn, hidx])
    vg = jnp.where(in_past, cache_v[li, phys, row, hidx], v_new[pn, hidx])
    return moba_attend(q, kg, vg, qpos, kpos, ok)


def moba_sample(q, k, v, page_table, cache_k, cache_v, li):
    return lax.map(lambda a: moba_sample_seq(a[0], a[1], a[2], a[3], cache_k, cache_v, li),
                   (q, k, v, page_table))


def gla_chunked(q, k, v, log_a, s0):
    b, t, h, dk = q.shape
    dv = v.shape[-1]
    c = min(GLA_CHUNK, t)
    n = -(-t // c)
    pad = n * c - t

    def blocks(a):
        a = jnp.pad(a, ((0, 0), (0, pad), (0, 0), (0, 0)))
        return jnp.moveaxis(a.reshape(b, n, c, h, a.shape[-1]), 1, 0)

    causal = jnp.tril(jnp.ones((c, c), dtype=bool))[None, :, :, None, None]

    def step(state, xs):
        qc, kc, vc, ac = xs
        qf = qc.astype(jnp.float32) * (dk ** -0.5)
        kf = kc.astype(jnp.float32)
        vf = vc.astype(jnp.float32)
        g = jnp.cumsum(ac.astype(jnp.float32), axis=1)
        o_inter = jnp.einsum('bihk,bhkv->bihv', qf * jnp.exp(g), state)
        decay = jnp.exp(jnp.where(causal, g[:, :, None] - g[:, None, :], -jnp.inf))
        att = jnp.einsum('bihk,bjhk,bijhk->bhij', qf, kf, decay)
        o_intra = jnp.einsum('bhij,bjhv->bihv', att, vf)
        g_last = g[:, -1]
        state = (jnp.exp(g_last)[..., None] * state
                 + jnp.einsum('bjhk,bjhv->bhkv', kf * jnp.exp(g_last[:, None] - g), vf))
        return state, o_inter + o_intra

    s_fin, o = lax.scan(step, s0.astype(jnp.float32), (blocks(q), blocks(k), blocks(v), blocks(log_a)))
    o = jnp.moveaxis(o, 0, 1).reshape(b, n * c, h, dv)[:, :t]
    return o, s_fin.astype(s0.dtype)


def hybrid_layer(x, pe, s0, moba_fn, g_ffn1, w_ffn1_gate, w_ffn1_up, w_ffn1_down, g_mix, w_in,
                 w_gate_b2, b_gate_b, g_gla_out, w_proj_a, w_proj_b, w_out, g_ffn2, w_ffn2_gate,
                 w_ffn2_up, w_ffn2_down, g_ple, w_ple_gate, w_ple_proj):
    b, t, _ = x.shape
    h = x + 0.5 * swiglu(rmsnorm(x, g_ffn1), w_ffn1_gate, w_ffn1_up, w_ffn1_down)
    u = rmsnorm(h, g_mix)
    qa, ka, va, qb, kb, vb, rb, zg, ga, gb = jnp.split(u @ w_in, _split_points(), axis=-1)
    qa = qa.reshape(b, t, A_HEADS, A_HEAD_DIM)
    ka = ka.reshape(b, t, A_HEADS, A_HEAD_DIM)
    va = va.reshape(b, t, A_HEADS, A_HEAD_DIM)
    oa = moba_fn(qa, ka, va).reshape(b, t, A_WIDTH) @ w_proj_a
    log_a = jax.nn.log_sigmoid((zg @ w_gate_b2 + b_gate_b).astype(jnp.float32)) / GATE_TEMP
    ob, s_fin = gla_chunked(qb.reshape(b, t, B_HEADS, B_KEY_DIM), kb.reshape(b, t, B_HEADS, B_KEY_DIM),
                            vb.reshape(b, t, B_HEADS, B_VAL_DIM), log_a.reshape(b, t, B_HEADS, B_KEY_DIM), s0)
    ob = rmsnorm(ob, g_gla_out).astype(x.dtype) * jax.nn.silu(rb).reshape(b, t, B_HEADS, B_VAL_DIM)
    ob = ob.reshape(b, t, BV_WIDTH) @ w_proj_b
    h = h + (jax.nn.sigmoid(ga) * oa + jax.nn.sigmoid(gb) * ob) @ w_out
    h = h + 0.5 * swiglu(rmsnorm(h, g_ffn2), w_ffn2_gate, w_ffn2_up, w_ffn2_down)
    h = h + jax.nn.sigmoid(rmsnorm(h, g_ple) @ w_ple_gate) * (pe @ w_ple_proj)
    return h, ka, va, s_fin


def setup_inputs(seed: int = 0) -> dict:
    key = jax.random.key(seed)
    ks = list(jax.random.split(key, 32))
    f32 = jnp.float32
    n_pages = PAST_LEN // PAGE_SIZE
    n_pool = (DEC_BATCH * n_pages * 5) // 4
    n_in = sum(IN_SPLITS)
    L = DEPTH

    def nrm(i, shape, scale):
        return scale * jax.random.normal(ks[i], shape, f32)

    def gain(i, shape):
        return 1.0 + 0.02 * jax.random.normal(ks[i], shape, f32)

    perm = jax.random.permutation(ks[0], n_pool)
    page_table = perm[:DEC_BATCH * n_pages].reshape(DEC_BATCH, n_pages).astype(jnp.int32)
    return {
        'x_prompt': nrm(1, (BATCH, SEQ, D_MODEL), 1.0),
        'x_sample': nrm(2, (DEC_BATCH, DEC_SEQ, D_MODEL), 1.0),
        'cache_k': nrm(3, (L, n_pool, PAGE_SIZE, A_HEADS, A_HEAD_DIM), 1.0),
        'cache_v': nrm(4, (L, n_pool, PAGE_SIZE, A_HEADS, A_HEAD_DIM), 1.0),
        'state_gla': nrm(5, (L, DEC_BATCH, B_HEADS, B_KEY_DIM, B_VAL_DIM), 0.5),
        'page_table': page_table,
        'p_prompt': nrm(6, (L, BATCH, SEQ, PLE_DIM), 1.0),
        'p_sample': nrm(7, (L, DEC_BATCH, DEC_SEQ, PLE_DIM), 1.0),
        'g_ffn1': gain(8, (L, D_MODEL)),
        'w_ffn1_gate': nrm(9, (L, D_MODEL, D_FF), D_MODEL ** -0.5),
        'w_ffn1_up': nrm(10, (L, D_MODEL, D_FF), D_MODEL ** -0.5),
        'w_ffn1_down': nrm(11, (L, D_FF, D_MODEL), D_FF ** -0.5),
        'g_mix': gain(12, (L, D_MODEL)),
        'w_in': nrm(13, (L, D_MODEL, n_in), D_MODEL ** -0.5),
        'w_gate_b2': nrm(14, (L, GATE_RANK, BK_WIDTH), GATE_RANK ** -0.5),
        'b_gate_b': nrm(15, (L, BK_WIDTH), 0.1),
        'g_gla_out': gain(16, (L, B_VAL_DIM)),
        'w_proj_a': nrm(17, (L, A_WIDTH, D_MODEL), A_WIDTH ** -0.5),
        'w_proj_b': nrm(18, (L, BV_WIDTH, D_MODEL), BV_WIDTH ** -0.5),
        'w_out': nrm(19, (L, D_MODEL, D_MODEL), D_MODEL ** -0.5),
        'g_ffn2': gain(20, (L, D_MODEL)),
        'w_ffn2_gate': nrm(21, (L, D_MODEL, D_FF), D_MODEL ** -0.5),
        'w_ffn2_up': nrm(22, (L, D_MODEL, D_FF), D_MODEL ** -0.5),
        'w_ffn2_down': nrm(23, (L, D_FF, D_MODEL), D_FF ** -0.5),
        'g_ple': gain(24, (L, D_MODEL)),
        'w_ple_gate': nrm(25, (L, D_MODEL, D_MODEL), D_MODEL ** -0.5),
        'w_ple_proj': nrm(26, (L, PLE_DIM, D_MODEL), PLE_DIM ** -0.5),
        'g_final': gain(27, (D_MODEL,)),
    }


def reference(x_prompt, x_sample, cache_k, cache_v, state_gla, page_table, p_prompt, p_sample,
              g_ffn1, w_ffn1_gate, w_ffn1_up, w_ffn1_down, g_mix, w_in, w_gate_b2, b_gate_b,
              g_gla_out, w_proj_a, w_proj_b, w_out, g_ffn2, w_ffn2_gate, w_ffn2_up, w_ffn2_down,
              g_ple, w_ple_gate, w_ple_proj, g_final):
    hp, hs = x_prompt, x_sample
    kp_l, vp_l, ks_l, vs_l, sp_l, ss_l = [], [], [], [], [], []
    s0_prompt = jnp.zeros((x_prompt.shape[0], B_HEADS, B_KEY_DIM, B_VAL_DIM), state_gla.dtype)
    prompt_moba = jax.vmap(moba_prompt_seq)
    for li in range(DEPTH):
        lw = (g_ffn1[li], w_ffn1_gate[li], w_ffn1_up[li], w_ffn1_down[li], g_mix[li], w_in[li],
              w_gate_b2[li], b_gate_b[li], g_gla_out[li], w_proj_a[li], w_proj_b[li], w_out[li],
              g_ffn2[li], w_ffn2_gate[li], w_ffn2_up[li], w_ffn2_down[li], g_ple[li],
              w_ple_gate[li], w_ple_proj[li])
        hp, kp, vp, sp = hybrid_layer(hp, p_prompt[li], s0_prompt, prompt_moba, *lw)
        sample_moba = lambda q, k, v, li=li: moba_sample(q, k, v, page_table, cache_k, cache_v, li)
        hs, ks_, vs_, ss = hybrid_layer(hs, p_sample[li], state_gla[li], sample_moba, *lw)
        kp_l.append(kp)
        vp_l.append(vp)
        ks_l.append(ks_)
        vs_l.append(vs_)
        sp_l.append(sp)
        ss_l.append(ss)
    y_prompt = rmsnorm(hp, g_final)
    y_sample = rmsnorm(hs, g_final)
    return (y_prompt, y_sample, jnp.stack(kp_l), jnp.stack(vp_l), jnp.stack(ks_l), jnp.stack(vs_l), jnp.stack(sp_l), jnp.stack(ss_l))
```

```python
import functools

import jax
import jax.numpy as jnp
from jax import lax
from jax.experimental import pallas as pl
from jax.experimental.pallas import tpu as pltpu

F32 = jnp.float32
BF16 = jnp.bfloat16
HIGHEST = lax.Precision.HIGHEST

EPS = 1e-6
A_HEADS = 16
A_HEAD_DIM = 128
A_WIDTH = A_HEADS * A_HEAD_DIM
MOBA_BLOCK = 256
MOBA_TOPK = 3
B_HEADS = 4
B_KEY_DIM = 256
B_VAL_DIM = 512
BK_WIDTH = B_HEADS * B_KEY_DIM
BV_WIDTH = B_HEADS * B_VAL_DIM
GATE_RANK = 16
GATE_TEMP = 16.0
GLA_CHUNK = 64
GLA_SUB = 16

LANES = 128
VMEM_LIMIT = 52 * 1024 * 1024
MASK_BIG = float(2 ** 30)

COL_QREL_HI = 32
COL_QREL_LO = 33
COL_ONE_HI = 34
COL_ONE_LO = 35


def _cparams(sem):
    return pltpu.CompilerParams(dimension_semantics=sem, vmem_limit_bytes=VMEM_LIMIT)


def _rmsnorm(x, g):
    return x * lax.rsqrt(jnp.mean(x * x, axis=-1, keepdims=True) + EPS) * g


def _tile(n, cap):
    if n <= cap:
        return n
    t = (cap // LANES) * LANES
    while n % t:
        t -= LANES
    return t


def _ffn_body(x_ref, g_ref, wg_ref, wu_ref, wd_ref, gn_ref, h_ref, u_ref, xn_ref):
    j = pl.program_id(1)

    @pl.when(j == 0)
    def _():
        xn_ref[...] = _rmsnorm(x_ref[...], g_ref[...]).astype(BF16)
        h_ref[...] = jnp.zeros_like(h_ref)

    xn = xn_ref[...]
    a = jnp.dot(xn, wg_ref[...], preferred_element_type=F32)
    b = jnp.dot(xn, wu_ref[...], preferred_element_type=F32)
    mid = (a * jax.nn.sigmoid(a) * b).astype(BF16)
    h_ref[...] += jnp.dot(mid, wd_ref[...], preferred_element_type=F32)

    @pl.when(j == pl.num_programs(1) - 1)
    def _():
        h = x_ref[...] + 0.5 * h_ref[...]
        h_ref[...] = h
        u_ref[...] = _rmsnorm(h, gn_ref[...]).astype(BF16)


def _ffn(x, g, wg, wu, wd, g_next):
    m, d = x.shape
    f = wg.shape[1]
    tm = _tile(m, 512)
    tf = _tile(f, 512)
    return pl.pallas_call(
        _ffn_body,
        grid=(m // tm, f // tf),
        in_specs=[
            pl.BlockSpec((tm, d), lambda i, j: (i, 0)),
            pl.BlockSpec((1, d), lambda i, j: (0, 0)),
            pl.BlockSpec((d, tf), lambda i, j: (0, j)),
            pl.BlockSpec((d, tf), lambda i, j: (0, j)),
            pl.BlockSpec((tf, d), lambda i, j: (j, 0)),
            pl.BlockSpec((1, d), lambda i, j: (0, 0)),
        ],
        out_specs=[
            pl.BlockSpec((tm, d), lambda i, j: (i, 0)),
            pl.BlockSpec((tm, d), lambda i, j: (i, 0)),
        ],
        out_shape=[jax.ShapeDtypeStruct((m, d), F32), jax.ShapeDtypeStruct((m, d), BF16)],
        scratch_shapes=[pltpu.VMEM((tm, d), BF16)],
        compiler_params=_cparams(("parallel", "arbitrary")),
        name="ffn",
    )(x, g, wg, wu, wd, g_next)


def _mm_body(x_ref, w_ref, o_ref):
    o_ref[...] = jnp.dot(x_ref[...], w_ref[...], preferred_element_type=F32).astype(o_ref.dtype)


def _mm_res_body(x_ref, w_ref, r_ref, o_ref):
    acc = jnp.dot(x_ref[...], w_ref[...], preferred_element_type=F32)
    o_ref[...] = (acc + r_ref[...]).astype(o_ref.dtype)


def _mm(x, w, out_dtype=F32, residual=None, name="mm"):
    m, k = x.shape
    n = w.shape[1]
    tm = _tile(m, 1024)
    tn = _tile(n, 1024)
    in_specs = [
        pl.BlockSpec((tm, k), lambda j, i: (i, 0)),
        pl.BlockSpec((k, tn), lambda j, i: (0, j)),
    ]
    args = [x, w]
    body = _mm_body
    if residual is not None:
        in_specs.append(pl.BlockSpec((tm, tn), lambda j, i: (i, j)))
        args.append(residual)
        body = _mm_res_body
    return pl.pallas_call(
        body,
        grid=(n // tn, m // tm),
        in_specs=in_specs,
        out_specs=pl.BlockSpec((tm, tn), lambda j, i: (i, j)),
        out_shape=jax.ShapeDtypeStruct((m, n), out_dtype),
        compiler_params=_cparams(("parallel", "parallel")),
        name=name,
    )(*args)


def _merge_body(u_ref, oa_ref, ob_ref, wga_ref, wgb_ref, wpa_ref, wpb_ref, o_ref):
    u = u_ref[...]
    ga = jnp.dot(u, wga_ref[...], preferred_element_type=F32)
    gb = jnp.dot(u, wgb_ref[...], preferred_element_type=F32)
    pa = jnp.dot(oa_ref[...].astype(BF16), wpa_ref[...], preferred_element_type=F32)
    pb = jnp.dot(ob_ref[...].astype(BF16), wpb_ref[...], preferred_element_type=F32)
    o_ref[...] = (jax.nn.sigmoid(ga) * pa + jax.nn.sigmoid(gb) * pb).astype(o_ref.dtype)


def _merge(u, oa, ob, wga, wgb, wpa, wpb):
    m, d = u.shape
    n = wga.shape[1]
    tm = _tile(m, 512)
    tn = _tile(n, 512)
    row = lambda i, j: (i, 0)
    col = lambda i, j: (0, j)
    return pl.pallas_call(
        _merge_body,
        grid=(m // tm, n // tn),
        in_specs=[
            pl.BlockSpec((tm, d), row),
            pl.BlockSpec((tm, oa.shape[1]), row),
            pl.BlockSpec((tm, ob.shape[1]), row),
            pl.BlockSpec((d, tn), col),
            pl.BlockSpec((d, tn), col),
            pl.BlockSpec((oa.shape[1], tn), col),
            pl.BlockSpec((ob.shape[1], tn), col),
        ],
        out_specs=pl.BlockSpec((tm, tn), lambda i, j: (i, j)),
        out_shape=jax.ShapeDtypeStruct((m, n), BF16),
        compiler_params=_cparams(("parallel", "parallel")),
        name="merge",
    )(u, oa, ob, wga, wgb, wpa, wpb)


def _ple_body(w_ref, h_ref, pe_ref, wpg_ref, wpp_ref, gf_ref, y_ref):
    gate = jax.nn.sigmoid(jnp.dot(w_ref[...], wpg_ref[...], preferred_element_type=F32))
    pp = jnp.dot(pe_ref[...].astype(BF16), wpp_ref[...], preferred_element_type=F32)
    y_ref[...] = _rmsnorm(h_ref[...] + gate * pp, gf_ref[...])


def _ple_final(w, h, pe, wpg, wpp, g_final):
    m, d = h.shape
    tm = _tile(m, 256)
    row = lambda i: (i, 0)
    whole = lambda i: (0, 0)
    return pl.pallas_call(
        _ple_body,
        grid=(m // tm,),
        in_specs=[
            pl.BlockSpec((tm, d), row),
            pl.BlockSpec((tm, d), row),
            pl.BlockSpec((tm, pe.shape[1]), row),
            pl.BlockSpec(wpg.shape, whole),
            pl.BlockSpec(wpp.shape, whole),
            pl.BlockSpec((1, d), whole),
        ],
        out_specs=pl.BlockSpec((tm, d), row),
        out_shape=jax.ShapeDtypeStruct((m, d), F32),
        compiler_params=_cparams(("parallel",)),
        name="ple_final",
    )(w, h, pe, wpg, wpp, g_final)


def _top3_mask(s, n_valid):
    rows, nb = s.shape
    lane = lax.broadcasted_iota(jnp.int32, (rows, nb), 1)
    lane_f = lane.astype(F32)
    neg = jnp.float32(-jnp.inf)
    s = jnp.where(lane < n_valid, s, neg)
    sel = jnp.zeros((rows, nb), F32)
    picks = []
    for _ in range(MOBA_TOPK):
        best = jnp.max(s, axis=-1, keepdims=True)
        idx = jnp.min(jnp.where(s == best, lane_f, float(nb)), axis=-1, keepdims=True)
        hit = (lane_f == idx) & (best > neg)
        sel = jnp.where(hit, 1.0, sel)
        s = jnp.where(hit, neg, s)
        picks.append(idx)
    return sel, picks


def _moba_prep_body(slope_ref, k_ref, v_ref, kx_ref, vb_ref, km_ref):
    h = pl.program_id(0)
    j = pl.program_id(1)
    k = k_ref[...]
    km_ref[0, pl.ds(j, 1), :] = jnp.sum(k, axis=0, keepdims=True) * (1.0 / MOBA_BLOCK)
    vb_ref[0] = v_ref[...].astype(BF16)
    slope = jnp.full((MOBA_BLOCK, LANES), slope_ref[h], F32)
    lane = lax.broadcasted_iota(jnp.int32, (MOBA_BLOCK, LANES), 1)
    krel = lax.broadcasted_iota(jnp.int32, (MOBA_BLOCK, LANES), 0).astype(F32)
    s_hi = slope.astype(BF16).astype(F32)
    s_lo = slope - s_hi
    sk = slope * krel
    sk_hi = sk.astype(BF16).astype(F32)
    sk_lo = sk - sk_hi
    ext = jnp.where(lane == j, 1.0, 0.0)
    ext = jnp.where(lane == COL_QREL_HI, -s_hi, ext)
    ext = jnp.where(lane == COL_QREL_LO, -s_lo, ext)
    ext = jnp.where(lane == COL_ONE_HI, sk_hi, ext)
    ext = jnp.where(lane == COL_ONE_LO, sk_lo, ext)
    kx_ref[0] = jnp.concatenate([k.astype(BF16), ext.astype(BF16)], axis=1)


def _moba_prompt_body(slope_ref, q_ref, kx_ref, vb_ref, km_ref, o_ref):
    h = pl.program_id(0)
    b = pl.program_id(1)
    nb = km_ref.shape[1]
    slope = slope_ref[h]
    q = q_ref[...]
    km = jnp.concatenate([km_ref[0], jnp.zeros((LANES - nb, A_HEAD_DIM), F32)], axis=0)
    s_sel = lax.dot_general(q, km, (((1,), (1,)), ((), ())),
                            precision=HIGHEST, preferred_element_type=F32)
    sel, _ = _top3_mask(s_sel, b)
    lane = lax.broadcasted_iota(jnp.int32, (MOBA_BLOCK, LANES), 1)
    sel = jnp.where(lane == b, 1.0, sel)
    ext = (sel - 1.0) * MASK_BIG
    qrel = lax.broadcasted_iota(jnp.int32, (MOBA_BLOCK, LANES), 0).astype(F32)
    ext = jnp.where(lane >= nb, 0.0, ext)
    ext = jnp.where((lane == COL_QREL_HI) | (lane == COL_QREL_LO), qrel, ext)
    ext = jnp.where((lane == COL_ONE_HI) | (lane == COL_ONE_LO), 1.0, ext)
    qx = jnp.concatenate([(q * (A_HEAD_DIM ** -0.5)).astype(BF16), ext.astype(BF16)], axis=1)

    def tile(j, causal):
        start = pl.multiple_of(j * MOBA_BLOCK, MOBA_BLOCK)
        kx = kx_ref[0, pl.ds(start, MOBA_BLOCK), :]
        s = lax.dot_general(qx, kx, (((1,), (1,)), ((), ())), preferred_element_type=F32)
        s = s - slope * ((b - j) * MOBA_BLOCK).astype(F32)
        if causal:
            r = lax.broadcasted_iota(jnp.int32, s.shape, 0)
            c = lax.broadcasted_iota(jnp.int32, s.shape, 1)
            s = jnp.where(c <= r, s, -MASK_BIG)
        return s, vb_ref[0, pl.ds(start, MOBA_BLOCK), :]

    s, v = tile(b, True)
    m = jnp.max(s, axis=-1, keepdims=True)
    p = jnp.exp(s - m)
    l = jnp.sum(p, axis=-1, keepdims=True)
    acc = jnp.dot(p.astype(BF16), v, preferred_element_type=F32)

    def step(j, carry):
        m, l, acc = carry
        s, v = tile(j, False)
        m_new = jnp.maximum(m, jnp.max(s, axis=-1, keepdims=True))
        alpha = jnp.exp(m - m_new)
        p = jnp.exp(s - m_new)
        l = alpha * l + jnp.sum(p, axis=-1, keepdims=True)
        acc = alpha * acc + jnp.dot(p.astype(BF16), v, preferred_element_type=F32)
        return m_new, l, acc

    m, l, acc = lax.fori_loop(0, b, step, (m, l, acc))
    o_ref[...] = (acc / l).astype(o_ref.dtype)


def _moba_prompt(q, k, v, slopes):
    s_len = q.shape[0]
    nb = s_len // MOBA_BLOCK
    assert nb * MOBA_BLOCK == s_len and MOBA_TOPK <= nb <= COL_QREL_HI
    dh = A_HEAD_DIM
    kx, vb, km = pl.pallas_call(
        _moba_prep_body,
        grid=(A_HEADS, nb),
        in_specs=[
            pl.BlockSpec(memory_space=pltpu.SMEM),
            pl.BlockSpec((MOBA_BLOCK, dh), lambda h, j: (j, h)),
            pl.BlockSpec((MOBA_BLOCK, dh), lambda h, j: (j, h)),
        ],
        out_specs=[
            pl.BlockSpec((1, MOBA_BLOCK, 2 * LANES), lambda h, j: (h, j, 0)),
            pl.BlockSpec((1, MOBA_BLOCK, dh), lambda h, j: (h, j, 0)),
            pl.BlockSpec((1, nb, dh), lambda h, j: (h, 0, 0)),
        ],
        out_shape=[
            jax.ShapeDtypeStruct((A_HEADS, s_len, 2 * LANES), BF16),
            jax.ShapeDtypeStruct((A_HEADS, s_len, dh), BF16),
            jax.ShapeDtypeStruct((A_HEADS, nb, dh), F32),
        ],
        compiler_params=_cparams(("parallel", "arbitrary")),
        name="moba_prep",
    )(slopes, k, v)
    return pl.pallas_call(
        _moba_prompt_body,
        grid=(A_HEADS, nb),
        in_specs=[
            pl.BlockSpec(memory_space=pltpu.SMEM),
            pl.BlockSpec((MOBA_BLOCK, dh), lambda h, b: (b, h)),
            pl.BlockSpec((1, s_len, 2 * LANES), lambda h, b: (h, 0, 0)),
            pl.BlockSpec((1, s_len, dh), lambda h, b: (h, 0, 0)),
            pl.BlockSpec((1, nb, dh), lambda h, b: (h, 0, 0)),
        ],
        out_specs=pl.BlockSpec((MOBA_BLOCK, dh), lambda h, b: (b, h)),
        out_shape=jax.ShapeDtypeStruct((s_len, A_WIDTH), BF16),
        compiler_params=_cparams(("parallel", "arbitrary")),
        name="moba_prompt",
    )(slopes, q, kx, vb, km)


def _page_sum_body(pt_ref, p0_ref, p1_ref, o_ref):
    j = pl.program_id(1)
    o_ref[0, pl.ds(j, 1), :] = (jnp.sum(p0_ref[0], axis=0, keepdims=True)
                                + jnp.sum(p1_ref[0], axis=0, keepdims=True))


def _page_sums(cache_k3, page_table, pages_per_block):
    assert pages_per_block == 2
    bsz, n_pages = page_table.shape
    _, page, width = cache_k3.shape
    nblk = n_pages // pages_per_block
    return pl.pallas_call(
        _page_sum_body,
        grid_spec=pltpu.PrefetchScalarGridSpec(
            num_scalar_prefetch=1,
            grid=(bsz, nblk),
            in_specs=[
                pl.BlockSpec((1, page, width), lambda s, j, pt: (pt[s, 2 * j], 0, 0)),
                pl.BlockSpec((1, page, width), lambda s, j, pt: (pt[s, 2 * j + 1], 0, 0)),
            ],
            out_specs=pl.BlockSpec((1, nblk, width), lambda s, j, pt: (s, 0, 0)),
        ),
        out_shape=jax.ShapeDtypeStruct((bsz, nblk, width), F32),
        compiler_params=_cparams(("parallel", "arbitrary")),
        name="page_sums",
    )(page_table, cache_k3, cache_k3)


def _sample_select_body(q_ref, kn_ref, ks_ref, o_ref, *, n_past_blocks):
    t = q_ref.shape[0]
    nbp = ks_ref.shape[1]
    lane = lax.broadcasted_iota(jnp.int32, (t, LANES), 1)
    first = lax.broadcasted_iota(jnp.int32, (8, A_HEAD_DIM), 0) == 0
    out = jnp.zeros((t, LANES), F32)
    for h in range(A_HEADS):
        hs = slice(h * A_HEAD_DIM, (h + 1) * A_HEAD_DIM)
        own = jnp.where(first, jnp.sum(kn_ref[:, hs], axis=0, keepdims=True), 0.0)
        ksum = jnp.concatenate([ks_ref[0, :, hs], own, jnp.zeros((LANES - nbp - 8, A_HEAD_DIM), F32)], axis=0)
        s = lax.dot_general(q_ref[:, hs], ksum * (1.0 / MOBA_BLOCK), (((1,), (1,)), ((), ())),
                            precision=HIGHEST, preferred_element_type=F32)
        _, picks = _top3_mask(s, n_past_blocks)
        for r, idx in enumerate(picks):
            out = jnp.where(lane == h * MOBA_TOPK + r, idx, out)
    o_ref[0] = out.astype(jnp.int32)


def _sample_select(q, k_new, ksum, t):
    bsz, nbp, width = ksum.shape
    assert nbp % 8 == 0 and nbp + 8 <= LANES and t <= 8
    return pl.pallas_call(
        functools.partial(_sample_select_body, n_past_blocks=nbp),
        grid=(bsz,),
        in_specs=[
            pl.BlockSpec((t, width), lambda s: (s, 0)),
            pl.BlockSpec((t, width), lambda s: (s, 0)),
            pl.BlockSpec((1, nbp, width), lambda s: (s, 0, 0)),
        ],
        out_specs=pl.BlockSpec((1, t, LANES), lambda s: (s, 0, 0)),
        out_shape=jax.ShapeDtypeStruct((bsz, t, LANES), jnp.int32),
        compiler_params=_cparams(("parallel",)),
        name="sample_select",
    )(q, k_new, ksum)


def _moba_sample_body(blk_ref, pt_ref, slope_ref, q_ref, kn_ref, vn_ref, ck_ref, cv_ref, o_ref,
                      kbuf, vbuf, sem, *, past, page):
    s = pl.program_id(0)
    h = pl.program_id(1)
    t = q_ref.shape[0]
    nsel = t * MOBA_TOPK
    ppb = MOBA_BLOCK // page

    def block_of(i):
        return blk_ref[s, (i // MOBA_TOPK) * (A_HEADS * MOBA_TOPK) + h * MOBA_TOPK + i % MOBA_TOPK]

    def copies(i, c):
        phys = pt_ref[s, block_of(i) * ppb + c]
        dst = pl.ds((i * ppb + c) * page, page)
        return (pltpu.make_async_copy(ck_ref.at[phys, :, h, :], kbuf.at[dst, :], sem.at[0]),
                pltpu.make_async_copy(cv_ref.at[phys, :, h, :], vbuf.at[dst, :], sem.at[1]))

    for i in range(nsel):
        for c in range(ppb):
            ck, cv = copies(i, c)
            ck.start()
            cv.start()

    slope = slope_ref[h]
    q = q_ref[...]
    qb = (q * (A_HEAD_DIM ** -0.5)).astype(BF16)
    ncol = nsel * MOBA_BLOCK
    row = lax.broadcasted_iota(jnp.int32, (t, ncol), 0)
    col = lax.broadcasted_iota(jnp.int32, (t, ncol), 1)
    blk_row = jnp.concatenate(
        [jnp.full((1, MOBA_BLOCK), block_of(i), jnp.int32) for i in range(nsel)], axis=1)
    kpos = blk_row * MOBA_BLOCK + (col & (MOBA_BLOCK - 1))
    qpos = past + row
    per_tok = MOBA_TOPK * MOBA_BLOCK
    mine = (col >= row * per_tok) & (col < (row + 1) * per_tok)
    bias = jnp.where(mine & (kpos <= qpos), -slope * (qpos - kpos).astype(F32), -jnp.inf)

    kn = kn_ref[...]
    r2 = lax.broadcasted_iota(jnp.int32, (t, t), 0)
    c2 = lax.broadcasted_iota(jnp.int32, (t, t), 1)
    s_own = lax.dot_general(qb, kn.astype(BF16), (((1,), (1,)), ((), ())), preferred_element_type=F32)
    s_own = jnp.where(c2 <= r2, s_own - slope * (r2 - c2).astype(F32), -jnp.inf)

    for i in range(nsel):
        for c in range(ppb):
            ck, cv = copies(i, c)
            ck.wait()
            cv.wait()

    s_past = lax.dot_general(qb, kbuf[...].astype(BF16), (((1,), (1,)), ((), ())),
                             preferred_element_type=F32) + bias
    m = jnp.maximum(jnp.max(s_past, axis=-1, keepdims=True), jnp.max(s_own, axis=-1, keepdims=True))
    p_past = jnp.exp(s_past - m)
    p_own = jnp.exp(s_own - m)
    l = jnp.sum(p_past, axis=-1, keepdims=True) + jnp.sum(p_own, axis=-1, keepdims=True)
    acc = jnp.dot(p_past.astype(BF16), vbuf[...].astype(BF16), preferred_element_type=F32)
    acc = acc + jnp.dot(p_own.astype(BF16), vn_ref[...].astype(BF16), preferred_element_type=F32)
    o_ref[...] = acc / l


def _moba_sample(q, k_new, v_new, blocks, page_table, cache_k, cache_v, slopes, t):
    bsz, n_pages = page_table.shape
    page = cache_k.shape[1]
    past = n_pages * page
    assert past % MOBA_BLOCK == 0 and MOBA_BLOCK % page == 0 and t <= MOBA_BLOCK
    dh = A_HEAD_DIM
    rows = t * MOBA_TOPK * MOBA_BLOCK
    tok = lambda s, h, *_: (s, h)
    return pl.pallas_call(
        functools.partial(_moba_sample_body, past=past, page=page),
        grid_spec=pltpu.PrefetchScalarGridSpec(
            num_scalar_prefetch=2,
            grid=(bsz, A_HEADS),
            in_specs=[
                pl.BlockSpec(memory_space=pltpu.SMEM),
                pl.BlockSpec((t, dh), tok),
                pl.BlockSpec((t, dh), tok),
                pl.BlockSpec((t, dh), tok),
                pl.BlockSpec(memory_space=pl.ANY),
                pl.BlockSpec(memory_space=pl.ANY),
            ],
            out_specs=pl.BlockSpec((t, dh), tok),
            scratch_shapes=[
                pltpu.VMEM((rows, dh), F32),
                pltpu.VMEM((rows, dh), F32),
                pltpu.SemaphoreType.DMA((2,)),
            ],
        ),
        out_shape=jax.ShapeDtypeStruct((bsz * t, A_WIDTH), F32),
        compiler_params=_cparams(("arbitrary", "arbitrary")),
        name="moba_sample",
    )(blocks[:, :, :A_HEADS * MOBA_TOPK].reshape(bsz, -1), page_table, slopes, q, k_new, v_new,
      cache_k, cache_v)


def _gla_body(q_ref, k_ref, v_ref, r_ref, z_ref, w2_ref, bg_ref, go_ref, s0_ref,
              o_ref, sf_ref, st_ref, *, t_blk):
    c = pl.program_id(2)
    cs = GLA_CHUNK
    dk = B_KEY_DIM

    @pl.when(c == 0)
    def _():
        st_ref[...] = s0_ref[0, 0].T

    def rows(ref):
        x = ref[...]
        if t_blk < cs:
            x = jnp.concatenate([x, jnp.zeros((cs - t_blk, x.shape[1]), x.dtype)], axis=0)
        return x

    q = rows(q_ref) * (dk ** -0.5)
    k = rows(k_ref)
    v = rows(v_ref)
    pre = jnp.dot(rows(z_ref), w2_ref[...], precision=HIGHEST, preferred_element_type=F32) + bg_ref[...]
    la = (jnp.minimum(pre, 0.0) - jnp.log1p(jnp.exp(-jnp.abs(pre)))) * (1.0 / GATE_TEMP)
    ri = lax.broadcasted_iota(jnp.int32, (cs, cs), 0)
    ci = lax.broadcasted_iota(jnp.int32, (cs, cs), 1)
    if t_blk < cs:
        la = jnp.where(lax.broadcasted_iota(jnp.int32, la.shape, 0) < t_blk, la, 0.0)
    g = jnp.dot((ci <= ri).astype(F32), la, precision=HIGHEST, preferred_element_type=F32)
    g_last = g[cs - 1:cs, :]

    nsub = cs // GLA_SUB
    refs = [jnp.zeros((GLA_SUB, dk), F32)]
    for i in range(1, nsub):
        refs.append(jnp.broadcast_to(g[i * GLA_SUB - 1:i * GLA_SUB, :], (GLA_SUB, dk)))
    a = (q * jnp.exp(g - jnp.concatenate(refs, axis=0))).astype(BF16)

    far = [jnp.zeros((GLA_SUB, cs), F32)]
    for i in range(1, nsub):
        r_i = g[i * GLA_SUB - 1:i * GLA_SUB, :]
        b_i = (k * jnp.exp(jnp.minimum(r_i - g, 0.0))).astype(BF16)
        far.append(lax.dot_general(a[i * GLA_SUB:(i + 1) * GLA_SUB], b_i, (((1,), (1,)), ((), ())),
                                   preferred_element_type=F32))
    att = jnp.where((ri // GLA_SUB) > (ci // GLA_SUB), jnp.concatenate(far, axis=0), 0.0)
    for d in range(GLA_SUB):
        if d == 0:
            w = q * k
        else:
            e = jnp.exp(jnp.minimum(g - pltpu.roll(g, d, 0), 0.0))
            w = q * pltpu.roll(k, d, 0) * e
        a_d = jnp.sum(w, axis=-1, keepdims=True)
        att = jnp.where((ri - ci == d) & (ri % GLA_SUB >= d), a_d, att)

    st = st_ref[...]
    o = lax.dot_general((q * jnp.exp(g)).astype(BF16), st.astype(BF16), (((1,), (1,)), ((), ())),
                        preferred_element_type=F32)
    vb = v.astype(BF16)
    o = o + jnp.dot(att.astype(BF16), vb, preferred_element_type=F32)
    kd = (k * jnp.exp(g_last - g)).astype(BF16)
    st_new = jnp.exp(g_last) * st + lax.dot_general(vb, kd, (((0,), (0,)), ((), ())),
                                                    preferred_element_type=F32)
    st_ref[...] = st_new

    @pl.when(c == pl.num_programs(2) - 1)
    def _():
        sf_ref[0, 0] = st_new.T

    rb = r_ref[...]
    on = _rmsnorm(o[:t_blk], go_ref[...]) * (rb * jax.nn.sigmoid(rb))
    o_ref[...] = on.astype(o_ref.dtype)


def _gla(proj, w2p, bg, g_out, s0, t, out_dtype):
    bsz = s0.shape[0]
    t_blk = min(t, GLA_CHUNK)
    assert t % t_blk == 0
    nc = t // t_blk
    dk, dv = B_KEY_DIM, B_VAL_DIM
    rowblk = lambda b, h, c: b * nc + c
    o, sf = pl.pallas_call(
        functools.partial(_gla_body, t_blk=t_blk),
        grid=(bsz, B_HEADS, nc),
        in_specs=[
            pl.BlockSpec((t_blk, dk), lambda b, h, c: (rowblk(b, h, c), h)),
            pl.BlockSpec((t_blk, dk), lambda b, h, c: (rowblk(b, h, c), BK_WIDTH // dk + h)),
            pl.BlockSpec((t_blk, dv), lambda b, h, c: (rowblk(b, h, c), 2 * BK_WIDTH // dv + h)),
            pl.BlockSpec((t_blk, dv), lambda b, h, c: (rowblk(b, h, c), (2 * BK_WIDTH + BV_WIDTH) // dv + h)),
            pl.BlockSpec((t_blk, LANES), lambda b, h, c: (rowblk(b, h, c), (2 * BK_WIDTH + 2 * BV_WIDTH) // LANES)),
            pl.BlockSpec((LANES, dk), lambda b, h, c: (0, h)),
            pl.BlockSpec((1, dk), lambda b, h, c: (0, h)),
            pl.BlockSpec((1, dv), lambda b, h, c: (0, 0)),
            pl.BlockSpec((1, 1, dk, dv), lambda b, h, c: (b, h, 0, 0)),
        ],
        out_specs=[
            pl.BlockSpec((t_blk, dv), lambda b, h, c: (rowblk(b, h, c), h)),
            pl.BlockSpec((1, 1, dk, dv), lambda b, h, c: (b, h, 0, 0)),
        ],
        out_shape=[
            jax.ShapeDtypeStruct((bsz * t, BV_WIDTH), out_dtype),
            jax.ShapeDtypeStruct(s0.shape, s0.dtype),
        ],
        scratch_shapes=[pltpu.VMEM((dv, dk), F32)],
        compiler_params=_cparams(("parallel", "parallel", "arbitrary")),
        name="gla",
    )(proj, proj, proj, proj, proj, w2p, bg, g_out, s0)
    return o, sf


def _layer(x, pe, s0, moba_fn, t, wts, mix_dtype):
    h1, u = _ffn(x, wts["g_ffn1"], wts["w1g"], wts["w1u"], wts["w1d"], wts["g_mix"])
    qa = _mm(u, wts["w_qa"], name="proj_qa")
    ka = _mm(u, wts["w_ka"], name="proj_ka")
    va = _mm(u, wts["w_va"], name="proj_va")
    gl = _mm(u, wts["w_gla"], name="proj_gla")
    oa = moba_fn(qa, ka, va)
    ob, s_fin = _gla(gl, wts["w_gate_b2"], wts["b_gate_b"], wts["g_gla_out"], s0, t, mix_dtype)
    mix = _merge(u, oa, ob, wts["w_ga"], wts["w_gb"], wts["w_proj_a"], wts["w_proj_b"])
    h2 = _mm(mix, wts["w_out"], residual=h1, name="out_proj")
    h3, w = _ffn(h2, wts["g_ffn2"], wts["w2g"], wts["w2u"], wts["w2d"], wts["g_ple"])
    y = _ple_final(w, h3, pe, wts["w_ple_gate"], wts["w_ple_proj"], wts["g_final"])
    return y, ka, va, s_fin


def kernel(x_prompt, x_sample, cache_k, cache_v, state_gla, page_table, p_prompt, p_sample, g_ffn1, w_ffn1_gate, w_ffn1_up, w_ffn1_down, g_mix, w_in, w_gate_b2, b_gate_b, g_gla_out, w_proj_a, w_proj_b, w_out, g_ffn2, w_ffn2_gate, w_ffn2_up, w_ffn2_down, g_ple, w_ple_gate, w_ple_proj, g_final):
    depth = w_in.shape[0]
    assert depth == 1, "single trunk layer"
    li = 0
    bp, sp, d = x_prompt.shape
    bs, ts, _ = x_sample.shape
    assert bp == 1
    bf = lambda a: a.astype(BF16)
    row = lambda a: a.reshape(1, -1)

    o_qa, o_ka, o_va = 0, A_WIDTH, 2 * A_WIDTH
    o_gla = 3 * A_WIDTH
    o_zg = o_gla + 2 * BK_WIDTH + 2 * BV_WIDTH
    o_ga = o_zg + GATE_RANK
    o_gb = o_ga + d
    wi = w_in[li]
    w_zg = jnp.pad(wi[:, o_zg:o_ga], ((0, 0), (0, LANES - GATE_RANK)))
    wts = dict(
        g_ffn1=row(g_ffn1[li]), w1g=bf(w_ffn1_gate[li]), w1u=bf(w_ffn1_up[li]), w1d=bf(w_ffn1_down[li]),
        g_mix=row(g_mix[li]),
        w_qa=bf(wi[:, o_qa:o_ka]), w_ka=bf(wi[:, o_ka:o_va]), w_va=bf(wi[:, o_va:o_gla]),
        w_gla=bf(jnp.concatenate([wi[:, o_gla:o_zg], w_zg], axis=1)),
        w_ga=bf(wi[:, o_ga:o_gb]), w_gb=bf(wi[:, o_gb:]),
        w_gate_b2=jnp.pad(w_gate_b2[li], ((0, LANES - GATE_RANK), (0, 0))),
        b_gate_b=row(b_gate_b[li]), g_gla_out=row(g_gla_out[li]),
        w_proj_a=bf(w_proj_a[li]), w_proj_b=bf(w_proj_b[li]), w_out=bf(w_out[li]),
        g_ffn2=row(g_ffn2[li]), w2g=bf(w_ffn2_gate[li]), w2u=bf(w_ffn2_up[li]), w2d=bf(w_ffn2_down[li]),
        g_ple=row(g_ple[li]), w_ple_gate=bf(w_ple_gate[li]), w_ple_proj=bf(w_ple_proj[li]),
        g_final=row(g_final),
    )
    slopes = jnp.exp2(-8.0 * jnp.arange(1, A_HEADS + 1, dtype=F32) / A_HEADS)

    s0_prompt = jnp.zeros((bp, B_HEADS, B_KEY_DIM, B_VAL_DIM), state_gla.dtype)
    moba_p = lambda q, k, v: _moba_prompt(q, k, v, slopes)
    yp, kp, vp, stp = _layer(x_prompt.reshape(sp, d), p_prompt[li].reshape(sp, -1), s0_prompt,
                             moba_p, sp, wts, BF16)

    n_pool, page = cache_k.shape[1], cache_k.shape[2]
    ck = cache_k.reshape(cache_k.shape[1:])
    cv = cache_v.reshape(cache_v.shape[1:])
    ppb = MOBA_BLOCK // page

    def moba_s(q, k, v):
        ksum = _page_sums(ck.reshape(n_pool, page, A_WIDTH), page_table, ppb)
        blocks = _sample_select(q, k, ksum, ts)
        return _moba_sample(q, k, v, blocks, page_table, ck, cv, slopes, ts)

    ys, ks, vs, sts = _layer(x_sample.reshape(bs * ts, d), p_sample[li].reshape(bs * ts, -1), state_gla[li],
                             moba_s, ts, wts, F32)

    heads = (A_HEADS, A_HEAD_DIM)
    return (yp.reshape(bp, sp, d), ys.reshape(bs, ts, d),
            kp.reshape(1, bp, sp, *heads), vp.reshape(1, bp, sp, *heads),
            ks.reshape(1, bs, ts, *heads), vs.reshape(1, bs, ts, *heads),
            stp[None], sts[None])
```

```python
import functools

import jax
import jax.numpy as jnp
from jax import lax
from jax.experimental import pallas as pl
from jax.experimental.pallas import tpu as pltpu

F32 = jnp.float32
BF16 = jnp.bfloat16
HIGHEST = lax.Precision.HIGHEST

EPS = 1e-6
A_HEADS = 16
A_HEAD_DIM = 128
A_WIDTH = A_HEADS * A_HEAD_DIM
MOBA_BLOCK = 256
MOBA_TOPK = 3
B_HEADS = 4
B_KEY_DIM = 256
B_VAL_DIM = 512
BK_WIDTH = B_HEADS * B_KEY_DIM
BV_WIDTH = B_HEADS * B_VAL_DIM
GATE_RANK = 16
GATE_TEMP = 16.0
GLA_CHUNK = 64
GLA_SUB = 16

LANES = 128
VMEM_LIMIT = 52 * 1024 * 1024
MASK_BIG = float(2 ** 30)

LOG2E = 1.4426950408889634
MOBA_GROUP = 4
PAGES_PER_STEP = 8

COL_KPOS = 32
COL_QPOS = 35


def _cparams(sem):
    return pltpu.CompilerParams(dimension_semantics=sem, vmem_limit_bytes=VMEM_LIMIT)


def _rmsnorm(x, g):
    return x * lax.rsqrt(jnp.mean(x * x, axis=-1, keepdims=True) + EPS) * g


def _tile(n, cap):
    if n <= cap:
        return n
    t = (cap // LANES) * LANES
    while n % t:
        t -= LANES
    return t


def _ffn_body(x_ref, g_ref, wg_ref, wu_ref, wd_ref, gn_ref, h_ref, u_ref, xn_ref):
    j = pl.program_id(1)

    @pl.when(j == 0)
    def _():
        xn_ref[...] = _rmsnorm(x_ref[...], g_ref[...]).astype(BF16)
        h_ref[...] = jnp.zeros_like(h_ref)

    xn = xn_ref[...]
    a = jnp.dot(xn, wg_ref[...], preferred_element_type=F32)
    b = jnp.dot(xn, wu_ref[...], preferred_element_type=F32)
    mid = (a * jax.nn.sigmoid(a) * b).astype(BF16)
    h_ref[...] += jnp.dot(mid, wd_ref[...], preferred_element_type=F32)

    @pl.when(j == pl.num_programs(1) - 1)
    def _():
        h = x_ref[...] + 0.5 * h_ref[...]
        h_ref[...] = h
        u_ref[...] = _rmsnorm(h, gn_ref[...]).astype(BF16)


def _ffn(x, g, wg, wu, wd, g_next):
    m, d = x.shape
    f = wg.shape[1]
    tm = _tile(m, 512)
    tf = _tile(f, 512)
    return pl.pallas_call(
        _ffn_body,
        grid=(m // tm, f // tf),
        in_specs=[
            pl.BlockSpec((tm, d), lambda i, j: (i, 0)),
            pl.BlockSpec((1, d), lambda i, j: (0, 0)),
            pl.BlockSpec((d, tf), lambda i, j: (0, j)),
            pl.BlockSpec((d, tf), lambda i, j: (0, j)),
            pl.BlockSpec((tf, d), lambda i, j: (j, 0)),
            pl.BlockSpec((1, d), lambda i, j: (0, 0)),
        ],
        out_specs=[
            pl.BlockSpec((tm, d), lambda i, j: (i, 0)),
            pl.BlockSpec((tm, d), lambda i, j: (i, 0)),
        ],
        out_shape=[jax.ShapeDtypeStruct((m, d), F32), jax.ShapeDtypeStruct((m, d), BF16)],
        scratch_shapes=[pltpu.VMEM((tm, d), BF16)],
        compiler_params=_cparams(("parallel", "arbitrary")),
        name="ffn",
    )(x, g, wg, wu, wd, g_next)


def _mm_body(x_ref, w_ref, o_ref):
    o_ref[...] = jnp.dot(x_ref[...], w_ref[...], preferred_element_type=F32).astype(o_ref.dtype)


def _mm_res_body(x_ref, w_ref, r_ref, o_ref):
    acc = jnp.dot(x_ref[...], w_ref[...], preferred_element_type=F32)
    o_ref[...] = (acc + r_ref[...]).astype(o_ref.dtype)


def _mm(x, w, out_dtype=F32, residual=None, name="mm"):
    m, k = x.shape
    n = w.shape[1]
    tm = _tile(m, 1024)
    tn = _tile(n, 1024)
    in_specs = [
        pl.BlockSpec((tm, k), lambda j, i: (i, 0)),
        pl.BlockSpec((k, tn), lambda j, i: (0, j)),
    ]
    args = [x, w]
    body = _mm_body
    if residual is not None:
        in_specs.append(pl.BlockSpec((tm, tn), lambda j, i: (i, j)))
        args.append(residual)
        body = _mm_res_body
    return pl.pallas_call(
        body,
        grid=(n // tn, m // tm),
        in_specs=in_specs,
        out_specs=pl.BlockSpec((tm, tn), lambda j, i: (i, j)),
        out_shape=jax.ShapeDtypeStruct((m, n), out_dtype),
        compiler_params=_cparams(("parallel", "parallel")),
        name=name,
    )(*args)


def _merge_body(u_ref, oa_ref, ob_ref, wga_ref, wgb_ref, wpa_ref, wpb_ref, o_ref):
    u = u_ref[...]
    ga = jnp.dot(u, wga_ref[...], preferred_element_type=F32)
    gb = jnp.dot(u, wgb_ref[...], preferred_element_type=F32)
    pa = jnp.dot(oa_ref[...].astype(BF16), wpa_ref[...], preferred_element_type=F32)
    pb = jnp.dot(ob_ref[...].astype(BF16), wpb_ref[...], preferred_element_type=F32)
    o_ref[...] = (jax.nn.sigmoid(ga) * pa + jax.nn.sigmoid(gb) * pb).astype(o_ref.dtype)


def _merge(u, oa, ob, wga, wgb, wpa, wpb):
    m, d = u.shape
    n = wga.shape[1]
    tm = _tile(m, 512)
    tn = _tile(n, 512)
    row = lambda i, j: (i, 0)
    col = lambda i, j: (0, j)
    return pl.pallas_call(
        _merge_body,
        grid=(m // tm, n // tn),
        in_specs=[
            pl.BlockSpec((tm, d), row),
            pl.BlockSpec((tm, oa.shape[1]), row),
            pl.BlockSpec((tm, ob.shape[1]), row),
            pl.BlockSpec((d, tn), col),
            pl.BlockSpec((d, tn), col),
            pl.BlockSpec((oa.shape[1], tn), col),
            pl.BlockSpec((ob.shape[1], tn), col),
        ],
        out_specs=pl.BlockSpec((tm, tn), lambda i, j: (i, j)),
        out_shape=jax.ShapeDtypeStruct((m, n), BF16),
        compiler_params=_cparams(("parallel", "parallel")),
        name="merge",
    )(u, oa, ob, wga, wgb, wpa, wpb)


def _ple_body(w_ref, h_ref, pe_ref, wpg_ref, wpp_ref, gf_ref, y_ref):
    gate = jax.nn.sigmoid(jnp.dot(w_ref[...], wpg_ref[...], preferred_element_type=F32))
    pp = jnp.dot(pe_ref[...].astype(BF16), wpp_ref[...], preferred_element_type=F32)
    y_ref[...] = _rmsnorm(h_ref[...] + gate * pp, gf_ref[...])


def _ple_final(w, h, pe, wpg, wpp, g_final):
    m, d = h.shape
    tm = _tile(m, 256)
    row = lambda i: (i, 0)
    whole = lambda i: (0, 0)
    return pl.pallas_call(
        _ple_body,
        grid=(m // tm,),
        in_specs=[
            pl.BlockSpec((tm, d), row),
            pl.BlockSpec((tm, d), row),
            pl.BlockSpec((tm, pe.shape[1]), row),
            pl.BlockSpec(wpg.shape, whole),
            pl.BlockSpec(wpp.shape, whole),
            pl.BlockSpec((1, d), whole),
        ],
        out_specs=pl.BlockSpec((tm, d), row),
        out_shape=jax.ShapeDtypeStruct((m, d), F32),
        compiler_params=_cparams(("parallel",)),
        name="ple_final",
    )(w, h, pe, wpg, wpp, g_final)


def _top3_mask(s, n_valid, axis):
    nb = s.shape[axis]
    blk = lax.broadcasted_iota(jnp.int32, s.shape, axis)
    lane_f = blk.astype(F32)
    neg = jnp.float32(-jnp.inf)
    s = jnp.where(blk < n_valid, s, neg)
    sel = jnp.zeros(s.shape, F32)
    picks = []
    for _ in range(MOBA_TOPK):
        best = jnp.max(s, axis=axis, keepdims=True)
        idx = jnp.min(jnp.where(s == best, lane_f, float(nb)), axis=axis, keepdims=True)
        hit = (lane_f == idx) & (best > neg)
        sel = jnp.where(hit, 1.0, sel)
        s = jnp.where(hit, neg, s)
        picks.append(idx)
    return sel, picks


def _split3(x):
    hi = x.astype(BF16).astype(F32)
    mid = (x - hi).astype(BF16).astype(F32)
    lo = (x - hi - mid).astype(BF16).astype(F32)
    return hi, mid, lo


def _moba_prep_body(slope_ref, k_ref, v_ref, kx_ref, vt_ref, km_ref):
    j = pl.program_id(0)

    @pl.when(j == 0)
    def _():
        km_ref[...] = jnp.zeros_like(km_ref)

    lane = lax.broadcasted_iota(jnp.int32, (MOBA_BLOCK, LANES), 1)
    kpos = (j * MOBA_BLOCK + lax.broadcasted_iota(jnp.int32, (MOBA_BLOCK, LANES), 0)).astype(F32)
    for h in range(A_HEADS):
        hs = slice(h * A_HEAD_DIM, (h + 1) * A_HEAD_DIM)
        k = k_ref[:, hs]
        km_ref[h, pl.ds(j, 1), :] = jnp.sum(k, axis=0, keepdims=True) * (1.0 / MOBA_BLOCK)
        vt_ref[h, 0] = v_ref[:, hs].T.astype(BF16)
        a_hi, a_mid, a_lo = _split3((slope_ref[h] * LOG2E) * kpos)
        ext = jnp.where(lane == j, 1.0, 0.0)
        ext = jnp.where(lane == COL_KPOS, a_hi, ext)
        ext = jnp.where(lane == COL_KPOS + 1, a_mid, ext)
        ext = jnp.where(lane == COL_KPOS + 2, a_lo, ext)
        ext = jnp.where((lane >= COL_QPOS) & (lane < COL_QPOS + 3), 1.0, ext)
        kx_ref[h, 0] = jnp.concatenate([k.astype(BF16), ext.astype(BF16)], axis=1)


def _moba_prompt_body(slope_ref, q_ref, kx_ref, vt_ref, km_ref, o_ref, s_buf):
    h = pl.program_id(0)
    b = pl.program_id(1)
    nb = kx_ref.shape[1]
    blk = MOBA_BLOCK
    qt = q_ref[...].T
    s_sel = jnp.dot(km_ref[0], qt, precision=HIGHEST, preferred_element_type=F32)
    sel, _ = _top3_mask(s_sel, b, axis=0)
    row = lax.broadcasted_iota(jnp.int32, (LANES, blk), 0)
    qpos = (b * blk + lax.broadcasted_iota(jnp.int32, (LANES, blk), 1)).astype(F32)
    c_hi, c_mid, c_lo = _split3((slope_ref[h] * -LOG2E) * qpos)
    ext = jnp.where(row == b, 0.0, (sel - 1.0) * MASK_BIG)
    ext = jnp.where(row >= nb, 0.0, ext)
    ext = jnp.where((row >= COL_KPOS) & (row < COL_KPOS + 3), 1.0, ext)
    ext = jnp.where(row == COL_QPOS, c_hi, ext)
    ext = jnp.where(row == COL_QPOS + 1, c_mid, ext)
    ext = jnp.where(row == COL_QPOS + 2, c_lo, ext)
    qx = jnp.concatenate([(qt * (A_HEAD_DIM ** -0.5 * LOG2E)).astype(BF16), ext.astype(BF16)], axis=0)

    kr = lax.broadcasted_iota(jnp.int32, (blk, blk), 0)
    qc = lax.broadcasted_iota(jnp.int32, (blk, blk), 1)
    last = b // MOBA_GROUP

    def logits(j, causal):
        s = jnp.dot(kx_ref[0, j], qx, preferred_element_type=F32)
        if causal:
            s = jnp.where(j * blk + kr <= b * blk + qc, s, -MASK_BIG)
        s_buf[j] = s
        return jnp.max(s, axis=0, keepdims=True)

    def group_max(g, m):
        for i in range(MOBA_GROUP):
            m = jnp.maximum(m, logits(g * MOBA_GROUP + i, False))
        return m

    m = lax.fori_loop(0, last, group_max, jnp.full((1, blk), -MASK_BIG, F32))
    for i in range(MOBA_GROUP):
        m = jnp.maximum(m, logits(last * MOBA_GROUP + i, True))

    def group_acc(g, carry):
        l, acc = carry
        for i in range(MOBA_GROUP):
            j = g * MOBA_GROUP + i
            p = jnp.exp2(s_buf[j] - m)
            l = l + jnp.sum(p, axis=0, keepdims=True)
            acc = acc + jnp.dot(vt_ref[0, j], p.astype(BF16), preferred_element_type=F32)
        return l, acc

    l, acc = lax.fori_loop(0, last + 1, group_acc,
                           (jnp.zeros((1, blk), F32), jnp.zeros((A_HEAD_DIM, blk), F32)))
    o_ref[...] = (acc / l).T.astype(o_ref.dtype)


def _moba_prompt(q, k, v, slopes):
    s_len = q.shape[0]
    nb = s_len // MOBA_BLOCK
    assert nb * MOBA_BLOCK == s_len and nb % MOBA_GROUP == 0 and MOBA_TOPK <= nb <= COL_KPOS
    dh = A_HEAD_DIM
    width = A_HEADS * dh
    kx, vt, km = pl.pallas_call(
        _moba_prep_body,
        grid=(nb,),
        in_specs=[
            pl.BlockSpec(memory_space=pltpu.SMEM),
            pl.BlockSpec((MOBA_BLOCK, width), lambda j: (j, 0)),
            pl.BlockSpec((MOBA_BLOCK, width), lambda j: (j, 0)),
        ],
        out_specs=[
            pl.BlockSpec((A_HEADS, 1, MOBA_BLOCK, 2 * LANES), lambda j: (0, j, 0, 0)),
            pl.BlockSpec((A_HEADS, 1, dh, MOBA_BLOCK), lambda j: (0, j, 0, 0)),
            pl.BlockSpec((A_HEADS, LANES, dh), lambda j: (0, 0, 0)),
        ],
        out_shape=[
            jax.ShapeDtypeStruct((A_HEADS, nb, MOBA_BLOCK, 2 * LANES), BF16),
            jax.ShapeDtypeStruct((A_HEADS, nb, dh, MOBA_BLOCK), BF16),
            jax.ShapeDtypeStruct((A_HEADS, LANES, dh), F32),
        ],
        compiler_params=_cparams(("arbitrary",)),
        name="moba_prep",
    )(slopes, k, v)
    return pl.pallas_call(
        _moba_prompt_body,
        grid=(A_HEADS, nb),
        in_specs=[
            pl.BlockSpec(memory_space=pltpu.SMEM),
            pl.BlockSpec((MOBA_BLOCK, dh), lambda h, b: (b, h)),
            pl.BlockSpec((1, nb, MOBA_BLOCK, 2 * LANES), lambda h, b: (h, 0, 0, 0)),
            pl.BlockSpec((1, nb, dh, MOBA_BLOCK), lambda h, b: (h, 0, 0, 0)),
            pl.BlockSpec((1, LANES, dh), lambda h, b: (h, 0, 0)),
        ],
        out_specs=pl.BlockSpec((MOBA_BLOCK, dh), lambda h, b: (b, h)),
        out_shape=jax.ShapeDtypeStruct((s_len, A_WIDTH), BF16),
        scratch_shapes=[pltpu.VMEM((nb, MOBA_BLOCK, MOBA_BLOCK), F32)],
        compiler_params=_cparams(("parallel", "arbitrary")),
        name="moba_prompt",
    )(slopes, q, kx, vt, km)


def _page_sum_body(page_refs, o_ref, pages_per_block):
    j = pl.program_id(1)
    nblk_step = len(page_refs) // pages_per_block
    for i in range(nblk_step):
        acc = jnp.sum(page_refs[i * pages_per_block][0], axis=0)
        for c in range(1, pages_per_block):
            acc = acc + jnp.sum(page_refs[i * pages_per_block + c][0], axis=0)
        o_ref[0, j * nblk_step + i] = acc


def _page_sums(cache_k, page_table, pages_per_block):
    bsz, n_pages = page_table.shape
    _, page, heads, dh = cache_k.shape
    nblk = n_pages // pages_per_block
    per_step = min(PAGES_PER_STEP, n_pages)
    assert n_pages % per_step == 0 and per_step % pages_per_block == 0

    def page_spec(c):
        return pl.BlockSpec((1, page, heads, dh), lambda s, j, pt: (pt[s, per_step * j + c], 0, 0, 0))

    def body(pt_ref, *refs):
        _page_sum_body(refs[:per_step], refs[per_step], pages_per_block)

    return pl.pallas_call(
        body,
        grid_spec=pltpu.PrefetchScalarGridSpec(
            num_scalar_prefetch=1,
            grid=(bsz, n_pages // per_step),
            in_specs=[page_spec(c) for c in range(per_step)],
            out_specs=pl.BlockSpec((1, nblk, heads, dh), lambda s, j, pt: (s, 0, 0, 0)),
        ),
        out_shape=jax.ShapeDtypeStruct((bsz, nblk, heads, dh), F32),
        compiler_params=_cparams(("parallel", "arbitrary")),
        name="page_sums",
    )(page_table, *([cache_k] * per_step))


def _sample_select_body(q_ref, kn_ref, ks_ref, o_ref, *, n_past_blocks):
    t = q_ref.shape[0]
    nbp = ks_ref.shape[1]
    lane = lax.broadcasted_iota(jnp.int32, (t, LANES), 1)
    first = lax.broadcasted_iota(jnp.int32, (8, A_HEAD_DIM), 0) == 0
    out = jnp.zeros((t, LANES), F32)
    for h in range(A_HEADS):
        hs = slice(h * A_HEAD_DIM, (h + 1) * A_HEAD_DIM)
        own = jnp.where(first, jnp.sum(kn_ref[:, hs], axis=0, keepdims=True), 0.0)
        ksum = jnp.concatenate([ks_ref[0, :, h, :], own, jnp.zeros((LANES - nbp - 8, A_HEAD_DIM), F32)], axis=0)
        s = lax.dot_general(q_ref[:, hs], ksum * (1.0 / MOBA_BLOCK), (((1,), (1,)), ((), ())),
                            precision=HIGHEST, preferred_element_type=F32)
        _, picks = _top3_mask(s, n_past_blocks, axis=1)
        for r, idx in enumerate(picks):
            out = jnp.where(lane == h * MOBA_TOPK + r, idx, out)
    o_ref[0] = out.astype(jnp.int32)


def _sample_select(q, k_new, ksum, t):
    bsz, nbp, heads, dh = ksum.shape
    width = heads * dh
    assert nbp % 8 == 0 and nbp + 8 <= LANES and t <= 8
    return pl.pallas_call(
        functools.partial(_sample_select_body, n_past_blocks=nbp),
        grid=(bsz,),
        in_specs=[
            pl.BlockSpec((t, width), lambda s: (s, 0)),
            pl.BlockSpec((t, width), lambda s: (s, 0)),
            pl.BlockSpec((1, nbp, heads, dh), lambda s: (s, 0, 0, 0)),
        ],
        out_specs=pl.BlockSpec((1, t, LANES), lambda s: (s, 0, 0)),
        out_shape=jax.ShapeDtypeStruct((bsz, t, LANES), jnp.int32),
        compiler_params=_cparams(("parallel",)),
        name="sample_select",
    )(q, k_new, ksum)


def _moba_sample_body(blk_ref, pt_ref, slope_ref, q_ref, kn_ref, vn_ref, ck_ref, cv_ref, o_ref,
                      kbuf, vbuf, sem, *, past, page):
    s = pl.program_id(0)
    h = pl.program_id(1)
    nh = pl.num_programs(1)
    step = s * nh + h
    slot = step % 2
    t = q_ref.shape[0]
    nsel = t * MOBA_TOPK
    ppb = MOBA_BLOCK // page

    def block_at(s_, h_, i):
        return blk_ref[s_, (i // MOBA_TOPK) * (A_HEADS * MOBA_TOPK) + h_ * MOBA_TOPK + i % MOBA_TOPK]

    def block_of(i):
        return block_at(s, h, i)

    def copies(s_, h_, slot_, i, c):
        phys = pt_ref[s_, block_at(s_, h_, i) * ppb + c]
        dst = pl.ds((i * ppb + c) * page, page)
        return (pltpu.make_async_copy(ck_ref.at[phys, :, h_, :], kbuf.at[slot_, dst, :], sem.at[slot_, 0]),
                pltpu.make_async_copy(cv_ref.at[phys, :, h_, :], vbuf.at[slot_, dst, :], sem.at[slot_, 1]))

    def start_all(s_, h_, slot_):
        for i in range(nsel):
            for c in range(ppb):
                ck, cv = copies(s_, h_, slot_, i, c)
                ck.start()
                cv.start()

    @pl.when(step == 0)
    def _():
        start_all(s, h, slot)

    nxt = step + 1

    @pl.when(nxt < pl.num_programs(0) * nh)
    def _():
        start_all(nxt // nh, nxt % nh, 1 - slot)

    slope = slope_ref[h]
    q = q_ref[...]
    qb = (q * (A_HEAD_DIM ** -0.5)).astype(BF16)
    ncol = nsel * MOBA_BLOCK
    row = lax.broadcasted_iota(jnp.int32, (t, ncol), 0)
    col = lax.broadcasted_iota(jnp.int32, (t, ncol), 1)
    blk_row = jnp.concatenate(
        [jnp.full((1, MOBA_BLOCK), block_of(i), jnp.int32) for i in range(nsel)], axis=1)
    kpos = blk_row * MOBA_BLOCK + (col & (MOBA_BLOCK - 1))
    qpos = past + row
    per_tok = MOBA_TOPK * MOBA_BLOCK
    mine = (col >= row * per_tok) & (col < (row + 1) * per_tok)
    bias = jnp.where(mine & (kpos <= qpos), -slope * (qpos - kpos).astype(F32), -jnp.inf)

    kn = kn_ref[...]
    r2 = lax.broadcasted_iota(jnp.int32, (t, t), 0)
    c2 = lax.broadcasted_iota(jnp.int32, (t, t), 1)
    s_own = lax.dot_general(qb, kn.astype(BF16), (((1,), (1,)), ((), ())), preferred_element_type=F32)
    s_own = jnp.where(c2 <= r2, s_own - slope * (r2 - c2).astype(F32), -jnp.inf)

    for i in range(nsel):
        for c in range(ppb):
            ck, cv = copies(s, h, slot, i, c)
            ck.wait()
            cv.wait()

    s_past = lax.dot_general(qb, kbuf[slot].astype(BF16), (((1,), (1,)), ((), ())),
                             preferred_element_type=F32) + bias
    m = jnp.maximum(jnp.max(s_past, axis=-1, keepdims=True), jnp.max(s_own, axis=-1, keepdims=True))
    p_past = jnp.exp(s_past - m)
    p_own = jnp.exp(s_own - m)
    l = jnp.sum(p_past, axis=-1, keepdims=True) + jnp.sum(p_own, axis=-1, keepdims=True)
    acc = jnp.dot(p_past.astype(BF16), vbuf[slot].astype(BF16), preferred_element_type=F32)
    acc = acc + jnp.dot(p_own.astype(BF16), vn_ref[...].astype(BF16), preferred_element_type=F32)
    o_ref[...] = acc / l


def _moba_sample(q, k_new, v_new, blocks, page_table, cache_k, cache_v, slopes, t):
    bsz, n_pages = page_table.shape
    page = cache_k.shape[1]
    past = n_pages * page
    assert past % MOBA_BLOCK == 0 and MOBA_BLOCK % page == 0 and t <= MOBA_BLOCK
    dh = A_HEAD_DIM
    rows = t * MOBA_TOPK * MOBA_BLOCK
    tok = lambda s, h, *_: (s, h)
    return pl.pallas_call(
        functools.partial(_moba_sample_body, past=past, page=page),
        grid_spec=pltpu.PrefetchScalarGridSpec(
            num_scalar_prefetch=2,
            grid=(bsz, A_HEADS),
            in_specs=[
                pl.BlockSpec(memory_space=pltpu.SMEM),
                pl.BlockSpec((t, dh), tok),
                pl.BlockSpec((t, dh), tok),
                pl.BlockSpec((t, dh), tok),
                pl.BlockSpec(memory_space=pl.ANY),
                pl.BlockSpec(memory_space=pl.ANY),
            ],
            out_specs=pl.BlockSpec((t, dh), tok),
            scratch_shapes=[
                pltpu.VMEM((2, rows, dh), F32),
                pltpu.VMEM((2, rows, dh), F32),
                pltpu.SemaphoreType.DMA((2, 2)),
            ],
        ),
        out_shape=jax.ShapeDtypeStruct((bsz * t, A_WIDTH), F32),
        compiler_params=_cparams(("arbitrary", "arbitrary")),
        name="moba_sample",
    )(blocks[:, :, :A_HEADS * MOBA_TOPK].reshape(bsz, -1), page_table, slopes, q, k_new, v_new,
      cache_k, cache_v)


def _gla_body(q_ref, k_ref, v_ref, r_ref, z_ref, w2_ref, bg_ref, go_ref, s0_ref,
              o_ref, sf_ref, st_ref, *, t_blk):
    c = pl.program_id(1)
    dk, dv = B_KEY_DIM, B_VAL_DIM

    @pl.when(c == 0)
    def _():
        for h in range(B_HEADS):
            st_ref[h] = s0_ref[0, h].T

    for h in range(B_HEADS):
        ks = slice(h * dk, (h + 1) * dk)
        vs = slice(h * dv, (h + 1) * dv)
        on, st_new = _gla_head(q_ref[:, ks], k_ref[:, ks], v_ref[:, vs], z_ref[...], r_ref[:, vs],
                               w2_ref[:, ks], bg_ref[:, ks], go_ref[...], st_ref[h], t_blk)
        st_ref[h] = st_new
        o_ref[:, vs] = on.astype(o_ref.dtype)

        @pl.when(c == pl.num_programs(1) - 1)
        def _():
            sf_ref[0, h] = st_new.T


def _gla_head(q, k, v, z, rb, w2, bg, go, st, t_blk):
    cs = GLA_CHUNK
    dk = B_KEY_DIM

    def rows(x):
        if t_blk < cs:
            x = jnp.concatenate([x, jnp.zeros((cs - t_blk, x.shape[1]), x.dtype)], axis=0)
        return x

    q = rows(q) * (dk ** -0.5)
    k = rows(k)
    v = rows(v)
    pre = jnp.dot(rows(z), w2, precision=HIGHEST, preferred_element_type=F32) + bg
    la = (jnp.minimum(pre, 0.0) - jnp.log1p(jnp.exp(-jnp.abs(pre)))) * (1.0 / GATE_TEMP)
    ri = lax.broadcasted_iota(jnp.int32, (cs, cs), 0)
    ci = lax.broadcasted_iota(jnp.int32, (cs, cs), 1)
    if t_blk < cs:
        la = jnp.where(lax.broadcasted_iota(jnp.int32, la.shape, 0) < t_blk, la, 0.0)
    g = jnp.dot((ci <= ri).astype(F32), la, precision=HIGHEST, preferred_element_type=F32)
    g_last = g[cs - 1:cs, :]

    nsub = cs // GLA_SUB
    refs = [jnp.zeros((GLA_SUB, dk), F32)]
    for i in range(1, nsub):
        refs.append(jnp.broadcast_to(g[i * GLA_SUB - 1:i * GLA_SUB, :], (GLA_SUB, dk)))
    a = (q * jnp.exp(g - jnp.concatenate(refs, axis=0))).astype(BF16)

    far = [jnp.zeros((GLA_SUB, cs), F32)]
    for i in range(1, nsub):
        r_i = g[i * GLA_SUB - 1:i * GLA_SUB, :]
        b_i = (k * jnp.exp(jnp.minimum(r_i - g, 0.0))).astype(BF16)
        far.append(lax.dot_general(a[i * GLA_SUB:(i + 1) * GLA_SUB], b_i, (((1,), (1,)), ((), ())),
                                   preferred_element_type=F32))
    att = jnp.where((ri // GLA_SUB) > (ci // GLA_SUB), jnp.concatenate(far, axis=0), 0.0)
    for d in range(GLA_SUB):
        if d == 0:
            w = q * k
        else:
            e = jnp.exp(jnp.minimum(g - pltpu.roll(g, d, 0), 0.0))
            w = q * pltpu.roll(k, d, 0) * e
        a_d = jnp.sum(w, axis=-1, keepdims=True)
        att = jnp.where((ri - ci == d) & (ri % GLA_SUB >= d), a_d, att)

    o = lax.dot_general((q * jnp.exp(g)).astype(BF16), st.astype(BF16), (((1,), (1,)), ((), ())),
                        preferred_element_type=F32)
    vb = v.astype(BF16)
    o = o + jnp.dot(att.astype(BF16), vb, preferred_element_type=F32)
    kd = (k * jnp.exp(g_last - g)).astype(BF16)
    st_new = jnp.exp(g_last) * st + lax.dot_general(vb, kd, (((0,), (0,)), ((), ())),
                                                    preferred_element_type=F32)
    on = _rmsnorm(o[:t_blk], go) * (rb * jax.nn.sigmoid(rb))
    return on, st_new


def _gla(proj, w2p, bg, g_out, s0, t, out_dtype):
    bsz = s0.shape[0]
    t_blk = min(t, GLA_CHUNK)
    assert t % t_blk == 0
    nc = t // t_blk
    dk, dv = B_KEY_DIM, B_VAL_DIM
    assert BV_WIDTH == 2 * BK_WIDTH
    rowblk = lambda b, c: b * nc + c
    o, sf = pl.pallas_call(
        functools.partial(_gla_body, t_blk=t_blk),
        grid=(bsz, nc),
        in_specs=[
            pl.BlockSpec((t_blk, BK_WIDTH), lambda b, c: (rowblk(b, c), 0)),
            pl.BlockSpec((t_blk, BK_WIDTH), lambda b, c: (rowblk(b, c), 1)),
            pl.BlockSpec((t_blk, BV_WIDTH), lambda b, c: (rowblk(b, c), 1)),
            pl.BlockSpec((t_blk, BV_WIDTH), lambda b, c: (rowblk(b, c), 2)),
            pl.BlockSpec((t_blk, LANES), lambda b, c: (rowblk(b, c), (2 * BK_WIDTH + 2 * BV_WIDTH) // LANES)),
            pl.BlockSpec((LANES, BK_WIDTH), lambda b, c: (0, 0)),
            pl.BlockSpec((1, BK_WIDTH), lambda b, c: (0, 0)),
            pl.BlockSpec((1, dv), lambda b, c: (0, 0)),
            pl.BlockSpec((1, B_HEADS, dk, dv), lambda b, c: (b, 0, 0, 0)),
        ],
        out_specs=[
            pl.BlockSpec((t_blk, BV_WIDTH), lambda b, c: (rowblk(b, c), 0)),
            pl.BlockSpec((1, B_HEADS, dk, dv), lambda b, c: (b, 0, 0, 0)),
        ],
        out_shape=[
            jax.ShapeDtypeStruct((bsz * t, BV_WIDTH), out_dtype),
            jax.ShapeDtypeStruct(s0.shape, s0.dtype),
        ],
        scratch_shapes=[pltpu.VMEM((B_HEADS, dv, dk), F32)],
        compiler_params=_cparams(("parallel", "arbitrary")),
        name="gla",
    )(proj, proj, proj, proj, proj, w2p, bg, g_out, s0)
    return o, sf


def _layer(x, pe, s0, moba_fn, t, wts, mix_dtype):
    h1, u = _ffn(x, wts["g_ffn1"], wts["w1g"], wts["w1u"], wts["w1d"], wts["g_mix"])
    qa = _mm(u, wts["w_qa"], name="proj_qa")
    ka = _mm(u, wts["w_ka"], name="proj_ka")
    va = _mm(u, wts["w_va"], name="proj_va")
    gl = _mm(u, wts["w_gla"], name="proj_gla")
    oa = moba_fn(qa, ka, va)
    ob, s_fin = _gla(gl, wts["w_gate_b2"], wts["b_gate_b"], wts["g_gla_out"], s0, t, mix_dtype)
    mix = _merge(u, oa, ob, wts["w_ga"], wts["w_gb"], wts["w_proj_a"], wts["w_proj_b"])
    h2 = _mm(mix, wts["w_out"], residual=h1, name="out_proj")
    h3, w = _ffn(h2, wts["g_ffn2"], wts["w2g"], wts["w2u"], wts["w2d"], wts["g_ple"])
    y = _ple_final(w, h3, pe, wts["w_ple_gate"], wts["w_ple_proj"], wts["g_final"])
    return y, ka, va, s_fin


def kernel(x_prompt, x_sample, cache_k, cache_v, state_gla, page_table, p_prompt, p_sample, g_ffn1, w_ffn1_gate, w_ffn1_up, w_ffn1_down, g_mix, w_in, w_gate_b2, b_gate_b, g_gla_out, w_proj_a, w_proj_b, w_out, g_ffn2, w_ffn2_gate, w_ffn2_up, w_ffn2_down, g_ple, w_ple_gate, w_ple_proj, g_final):
    depth = w_in.shape[0]
    assert depth == 1, "single trunk layer"
    li = 0
    bp, sp, d = x_prompt.shape
    bs, ts, _ = x_sample.shape
    assert bp == 1
    bf = lambda a: a.astype(BF16)
    row = lambda a: a.reshape(1, -1)

    o_qa, o_ka, o_va = 0, A_WIDTH, 2 * A_WIDTH
    o_gla = 3 * A_WIDTH
    o_zg = o_gla + 2 * BK_WIDTH + 2 * BV_WIDTH
    o_ga = o_zg + GATE_RANK
    o_gb = o_ga + d
    wi = w_in[li]
    w_zg = jnp.pad(wi[:, o_zg:o_ga], ((0, 0), (0, LANES - GATE_RANK)))
    wts = dict(
        g_ffn1=row(g_ffn1[li]), w1g=bf(w_ffn1_gate[li]), w1u=bf(w_ffn1_up[li]), w1d=bf(w_ffn1_down[li]),
        g_mix=row(g_mix[li]),
        w_qa=bf(wi[:, o_qa:o_ka]), w_ka=bf(wi[:, o_ka:o_va]), w_va=bf(wi[:, o_va:o_gla]),
        w_gla=bf(jnp.concatenate([wi[:, o_gla:o_zg], w_zg], axis=1)),
        w_ga=bf(wi[:, o_ga:o_gb]), w_gb=bf(wi[:, o_gb:]),
        w_gate_b2=jnp.pad(w_gate_b2[li], ((0, LANES - GATE_RANK), (0, 0))),
        b_gate_b=row(b_gate_b[li]), g_gla_out=row(g_gla_out[li]),
        w_proj_a=bf(w_proj_a[li]), w_proj_b=bf(w_proj_b[li]), w_out=bf(w_out[li]),
        g_ffn2=row(g_ffn2[li]), w2g=bf(w_ffn2_gate[li]), w2u=bf(w_ffn2_up[li]), w2d=bf(w_ffn2_down[li]),
        g_ple=row(g_ple[li]), w_ple_gate=bf(w_ple_gate[li]), w_ple_proj=bf(w_ple_proj[li]),
        g_final=row(g_final),
    )
    slopes = jnp.exp2(-8.0 * jnp.arange(1, A_HEADS + 1, dtype=F32) / A_HEADS)

    s0_prompt = jnp.zeros((bp, B_HEADS, B_KEY_DIM, B_VAL_DIM), state_gla.dtype)
    moba_p = lambda q, k, v: _moba_prompt(q, k, v, slopes)
    yp, kp, vp, stp = _layer(x_prompt.reshape(sp, d), p_prompt[li].reshape(sp, -1), s0_prompt,
                             moba_p, sp, wts, BF16)

    n_pool, page = cache_k.shape[1], cache_k.shape[2]
    ck = cache_k.reshape(cache_k.shape[1:])
    cv = cache_v.reshape(cache_v.shape[1:])
    ppb = MOBA_BLOCK // page

    def moba_s(q, k, v):
        ksum = _page_sums(ck, page_table, ppb)
        blocks = _sample_select(q, k, ksum, ts)
        return _moba_sample(q, k, v, blocks, page_table, ck, cv, slopes, ts)

    ys, ks, vs, sts = _layer(x_sample.reshape(bs * ts, d), p_sample[li].reshape(bs * ts, -1), state_gla[li],
                             moba_s, ts, wts, F32)

    heads = (A_HEADS, A_HEAD_DIM)
    return (yp.reshape(bp, sp, d), ys.reshape(bs, ts, d),
            kp.reshape(1, bp, sp, *heads), vp.reshape(1, bp, sp, *heads),
            ks.reshape(1, bs, ts, *heads), vs.reshape(1, bs, ts, *heads),
            stp[None], sts[None])
```

```python
import functools

import jax
import jax.numpy as jnp
from jax import lax
from jax.experimental import pallas as pl
from jax.experimental.pallas import tpu as pltpu

F32 = jnp.float32
BF16 = jnp.bfloat16
HIGHEST = lax.Precision.HIGHEST

EPS = 1e-6
A_HEADS = 16
A_HEAD_DIM = 128
A_WIDTH = A_HEADS * A_HEAD_DIM
MOBA_BLOCK = 256
MOBA_TOPK = 3
B_HEADS = 4
B_KEY_DIM = 256
B_VAL_DIM = 512
BK_WIDTH = B_HEADS * B_KEY_DIM
BV_WIDTH = B_HEADS * B_VAL_DIM
GATE_RANK = 16
GATE_TEMP = 16.0
GLA_CHUNK = 64
GLA_SUB = 16

LANES = 128
VMEM_LIMIT = 52 * 1024 * 1024
MASK_BIG = float(2 ** 30)

LOG2E = 1.4426950408889634
MOBA_GROUP = 4

SEL_ROWS = 32
COL_KPOS = SEL_ROWS
COL_QPOS = COL_KPOS + 3


def _cparams(sem):
    return pltpu.CompilerParams(dimension_semantics=sem, vmem_limit_bytes=VMEM_LIMIT)


def _rmsnorm(x, g):
    return x * lax.rsqrt(jnp.mean(x * x, axis=-1, keepdims=True) + EPS) * g


def _tile(n, cap):
    if n <= cap:
        return n
    t = (cap // LANES) * LANES
    while n % t:
        t -= LANES
    return t


def _ffn_body(x_ref, g_ref, wg_ref, wu_ref, wd_ref, gn_ref, h_ref, u_ref, xn_ref):
    j = pl.program_id(1)

    @pl.when(j == 0)
    def _():
        xn_ref[...] = _rmsnorm(x_ref[...], g_ref[...]).astype(BF16)
        h_ref[...] = jnp.zeros_like(h_ref)

    xn = xn_ref[...]
    a = jnp.dot(xn, wg_ref[...], preferred_element_type=F32)
    b = jnp.dot(xn, wu_ref[...], preferred_element_type=F32)
    mid = (a * jax.nn.sigmoid(a) * b).astype(BF16)
    h_ref[...] += jnp.dot(mid, wd_ref[...], preferred_element_type=F32)

    @pl.when(j == pl.num_programs(1) - 1)
    def _():
        h = x_ref[...] + 0.5 * h_ref[...]
        h_ref[...] = h
        u_ref[...] = _rmsnorm(h, gn_ref[...]).astype(BF16)


def _ffn(x, g, wg, wu, wd, g_next):
    m, d = x.shape
    f = wg.shape[1]
    tm = _tile(m, 512)
    tf = _tile(f, 512)
    return pl.pallas_call(
        _ffn_body,
        grid=(m // tm, f // tf),
        in_specs=[
            pl.BlockSpec((tm, d), lambda i, j: (i, 0)),
            pl.BlockSpec((1, d), lambda i, j: (0, 0)),
            pl.BlockSpec((d, tf), lambda i, j: (0, j)),
            pl.BlockSpec((d, tf), lambda i, j: (0, j)),
            pl.BlockSpec((tf, d), lambda i, j: (j, 0)),
            pl.BlockSpec((1, d), lambda i, j: (0, 0)),
        ],
        out_specs=[
            pl.BlockSpec((tm, d), lambda i, j: (i, 0)),
            pl.BlockSpec((tm, d), lambda i, j: (i, 0)),
        ],
        out_shape=[jax.ShapeDtypeStruct((m, d), F32), jax.ShapeDtypeStruct((m, d), BF16)],
        scratch_shapes=[pltpu.VMEM((tm, d), BF16)],
        compiler_params=_cparams(("parallel", "arbitrary")),
        name="ffn",
    )(x, g, wg, wu, wd, g_next)


def _mm_body(x_ref, w_ref, o_ref):
    o_ref[...] = jnp.dot(x_ref[...], w_ref[...], preferred_element_type=F32).astype(o_ref.dtype)


def _mm_res_body(x_ref, w_ref, r_ref, o_ref):
    acc = jnp.dot(x_ref[...], w_ref[...], preferred_element_type=F32)
    o_ref[...] = (acc + r_ref[...]).astype(o_ref.dtype)


def _mm(x, w, out_dtype=F32, residual=None, name="mm"):
    m, k = x.shape
    n = w.shape[1]
    tm = _tile(m, 1024)
    tn = _tile(n, 1024)
    in_specs = [
        pl.BlockSpec((tm, k), lambda j, i: (i, 0)),
        pl.BlockSpec((k, tn), lambda j, i: (0, j)),
    ]
    args = [x, w]
    body = _mm_body
    if residual is not None:
        in_specs.append(pl.BlockSpec((tm, tn), lambda j, i: (i, j)))
        args.append(residual)
        body = _mm_res_body
    return pl.pallas_call(
        body,
        grid=(n // tn, m // tm),
        in_specs=in_specs,
        out_specs=pl.BlockSpec((tm, tn), lambda j, i: (i, j)),
        out_shape=jax.ShapeDtypeStruct((m, n), out_dtype),
        compiler_params=_cparams(("parallel", "parallel")),
        name=name,
    )(*args)


def _merge_body(u_ref, oa_ref, ob_ref, wga_ref, wgb_ref, wpa_ref, wpb_ref, o_ref):
    u = u_ref[...]
    ga = jnp.dot(u, wga_ref[...], preferred_element_type=F32)
    gb = jnp.dot(u, wgb_ref[...], preferred_element_type=F32)
    pa = jnp.dot(oa_ref[...].astype(BF16), wpa_ref[...], preferred_element_type=F32)
    pb = jnp.dot(ob_ref[...].astype(BF16), wpb_ref[...], preferred_element_type=F32)
    o_ref[...] = (jax.nn.sigmoid(ga) * pa + jax.nn.sigmoid(gb) * pb).astype(o_ref.dtype)


def _merge(u, oa, ob, wga, wgb, wpa, wpb):
    m, d = u.shape
    n = wga.shape[1]
    tm = _tile(m, 512)
    tn = _tile(n, 512)
    row = lambda i, j: (i, 0)
    col = lambda i, j: (0, j)
    return pl.pallas_call(
        _merge_body,
        grid=(m // tm, n // tn),
        in_specs=[
            pl.BlockSpec((tm, d), row),
            pl.BlockSpec((tm, oa.shape[1]), row),
            pl.BlockSpec((tm, ob.shape[1]), row),
            pl.BlockSpec((d, tn), col),
            pl.BlockSpec((d, tn), col),
            pl.BlockSpec((oa.shape[1], tn), col),
            pl.BlockSpec((ob.shape[1], tn), col),
        ],
        out_specs=pl.BlockSpec((tm, tn), lambda i, j: (i, j)),
        out_shape=jax.ShapeDtypeStruct((m, n), BF16),
        compiler_params=_cparams(("parallel", "parallel")),
        name="merge",
    )(u, oa, ob, wga, wgb, wpa, wpb)


def _ple_body(w_ref, h_ref, pe_ref, wpg_ref, wpp_ref, gf_ref, y_ref):
    gate = jax.nn.sigmoid(jnp.dot(w_ref[...], wpg_ref[...], preferred_element_type=F32))
    pp = jnp.dot(pe_ref[...].astype(BF16), wpp_ref[...], preferred_element_type=F32)
    y_ref[...] = _rmsnorm(h_ref[...] + gate * pp, gf_ref[...])


def _ple_final(w, h, pe, wpg, wpp, g_final):
    m, d = h.shape
    tm = _tile(m, 256)
    row = lambda i: (i, 0)
    whole = lambda i: (0, 0)
    return pl.pallas_call(
        _ple_body,
        grid=(m // tm,),
        in_specs=[
            pl.BlockSpec((tm, d), row),
            pl.BlockSpec((tm, d), row),
            pl.BlockSpec((tm, pe.shape[1]), row),
            pl.BlockSpec(wpg.shape, whole),
            pl.BlockSpec(wpp.shape, whole),
            pl.BlockSpec((1, d), whole),
        ],
        out_specs=pl.BlockSpec((tm, d), row),
        out_shape=jax.ShapeDtypeStruct((m, d), F32),
        compiler_params=_cparams(("parallel",)),
        name="ple_final",
    )(w, h, pe, wpg, wpp, g_final)


def _top3_mask(s, n_valid, axis):
    nb = s.shape[axis]
    blk = lax.broadcasted_iota(jnp.int32, s.shape, axis)
    lane_f = blk.astype(F32)
    neg = jnp.float32(-jnp.inf)
    s = jnp.where(blk < n_valid, s, neg)
    sel = jnp.zeros(s.shape, F32)
    picks = []
    for _ in range(MOBA_TOPK):
        best = jnp.max(s, axis=axis, keepdims=True)
        idx = jnp.min(jnp.where(s == best, lane_f, float(nb)), axis=axis, keepdims=True)
        hit = (lane_f == idx) & (best > neg)
        sel = jnp.where(hit, 1.0, sel)
        s = jnp.where(hit, neg, s)
        picks.append(idx)
    return sel, picks


def _split3(x):
    hi = x.astype(BF16).astype(F32)
    mid = (x - hi).astype(BF16).astype(F32)
    lo = (x - hi - mid).astype(BF16).astype(F32)
    return hi, mid, lo


def _moba_prep_body(slope_ref, k_ref, v_ref, kx_ref, vt_ref, km_ref):
    j = pl.program_id(0)

    @pl.when(j == 0)
    def _():
        km_ref[...] = jnp.zeros_like(km_ref)

    lane = lax.broadcasted_iota(jnp.int32, (MOBA_BLOCK, LANES), 1)
    kpos = (j * MOBA_BLOCK + lax.broadcasted_iota(jnp.int32, (MOBA_BLOCK, LANES), 0)).astype(F32)
    for h in range(A_HEADS):
        hs = slice(h * A_HEAD_DIM, (h + 1) * A_HEAD_DIM)
        k = k_ref[:, hs]
        km_ref[h, pl.ds(j, 1), :] = jnp.sum(k, axis=0, keepdims=True) * (1.0 / MOBA_BLOCK)
        vt_ref[h, 0] = v_ref[:, hs].T.astype(BF16)
        a_hi, a_mid, a_lo = _split3((slope_ref[h] * LOG2E) * kpos)
        ext = jnp.where(lane == j, 1.0, 0.0)
        ext = jnp.where(lane == COL_KPOS, a_hi, ext)
        ext = jnp.where(lane == COL_KPOS + 1, a_mid, ext)
        ext = jnp.where(lane == COL_KPOS + 2, a_lo, ext)
        ext = jnp.where((lane >= COL_QPOS) & (lane < COL_QPOS + 3), 1.0, ext)
        kx_ref[h, 0] = jnp.concatenate([k.astype(BF16), ext.astype(BF16)], axis=1)


def _dot_bf16x3(a, b):
    a_hi = a.astype(BF16)
    a_lo = (a - a_hi.astype(F32)).astype(BF16)
    b_hi = b.astype(BF16)
    b_lo = (b - b_hi.astype(F32)).astype(BF16)
    return jnp.dot(jnp.concatenate([a_hi, a_hi, a_lo], axis=1), jnp.concatenate([b_hi, b_lo, b_hi], axis=0),
                   preferred_element_type=F32)


def _moba_prompt_body(pt_ref, slope_ref, q_ref, kx_ref, vt_ref, km_ref, ck_ref, o_ref, ks_ref,
                      s_buf, page_buf, sem, *, n_pages, pages_per_block):
    h = pl.program_id(0)
    b = pl.program_id(1)
    nb = kx_ref.shape[1]
    blk = MOBA_BLOCK
    step = h * nb + b
    slot = step % 2
    per_step = page_buf.shape[1]

    seq_steps = n_pages // per_step

    def page_copies(step_, slot_):
        seq = step_ // seq_steps
        first = (step_ - seq * seq_steps) * per_step
        return [pltpu.make_async_copy(ck_ref.at[pt_ref[seq, first + c]], page_buf.at[slot_, c], sem.at[slot_])
                for c in range(per_step)]

    @pl.when(step == 0)
    def _():
        for cp in page_copies(step, slot):
            cp.start()

    @pl.when(step + 1 < pl.num_programs(0) * nb)
    def _():
        for cp in page_copies(step + 1, 1 - slot):
            cp.start()

    qt = q_ref[...].T
    s_sel = _dot_bf16x3(km_ref[0], qt)
    sel, _ = _top3_mask(s_sel, b, axis=0)
    row = lax.broadcasted_iota(jnp.int32, (SEL_ROWS, blk), 0)
    pen = jnp.where((row == b) | (row >= nb), 0.0, (sel - 1.0) * MASK_BIG)
    row = lax.broadcasted_iota(jnp.int32, (LANES - SEL_ROWS, blk), 0) + SEL_ROWS
    qpos = (b * blk + lax.broadcasted_iota(jnp.int32, (LANES - SEL_ROWS, blk), 1)).astype(F32)
    c_hi, c_mid, c_lo = _split3((slope_ref[h] * -LOG2E) * qpos)
    ext = jnp.where((row >= COL_KPOS) & (row < COL_KPOS + 3), 1.0, 0.0)
    ext = jnp.where(row == COL_QPOS, c_hi, ext)
    ext = jnp.where(row == COL_QPOS + 1, c_mid, ext)
    ext = jnp.where(row == COL_QPOS + 2, c_lo, ext)
    qx = jnp.concatenate([(qt * (A_HEAD_DIM ** -0.5 * LOG2E)).astype(BF16), pen.astype(BF16),
                          ext.astype(BF16)], axis=0)

    kr = lax.broadcasted_iota(jnp.int32, (blk, blk), 0)
    qc = lax.broadcasted_iota(jnp.int32, (blk, blk), 1)
    last = b // MOBA_GROUP

    def logits(j, causal):
        s = jnp.dot(kx_ref[0, j], qx, preferred_element_type=F32)
        if causal:
            s = jnp.where(j * blk + kr <= b * blk + qc, s, -MASK_BIG)
        s_buf[j] = s
        return jnp.max(s, axis=0, keepdims=True)

    def group_max(g, m):
        for i in range(MOBA_GROUP):
            m = jnp.maximum(m, logits(g * MOBA_GROUP + i, False))
        return m

    m = lax.fori_loop(0, last, group_max, jnp.full((1, blk), -MASK_BIG, F32))
    for i in range(MOBA_GROUP):
        m = jnp.maximum(m, logits(last * MOBA_GROUP + i, True))

    def group_acc(g, carry):
        l, acc = carry
        for i in range(MOBA_GROUP):
            j = g * MOBA_GROUP + i
            p = jnp.exp2(s_buf[j] - m)
            l = l + jnp.sum(p, axis=0, keepdims=True)
            acc = acc + jnp.dot(vt_ref[0, j], p.astype(BF16), preferred_element_type=F32)
        return l, acc

    l, acc = lax.fori_loop(0, last + 1, group_acc,
                           (jnp.zeros((1, blk), F32), jnp.zeros((A_HEAD_DIM, blk), F32)))
    o_ref[...] = (acc / l).T.astype(o_ref.dtype)

    pltpu.make_async_copy(page_buf.at[slot], page_buf.at[slot], sem.at[slot]).wait()
    for i in range(per_step // pages_per_block):
        total = jnp.sum(page_buf[slot, i * pages_per_block], axis=0)
        for c in range(1, pages_per_block):
            total = total + jnp.sum(page_buf[slot, i * pages_per_block + c], axis=0)
        ks_ref[0, i] = total


def _moba_prompt(q, k, v, slopes, cache_k, page_table):
    s_len = q.shape[0]
    nb = s_len // MOBA_BLOCK
    assert nb * MOBA_BLOCK == s_len and nb % MOBA_GROUP == 0 and MOBA_TOPK <= nb <= SEL_ROWS
    dh = A_HEAD_DIM
    width = A_HEADS * dh
    bsz, n_pages = page_table.shape
    page = cache_k.shape[1]
    ppb = MOBA_BLOCK // page
    nsteps = A_HEADS * nb
    per_step = bsz * n_pages // nsteps
    blk_step = per_step // ppb
    assert per_step * nsteps == bsz * n_pages and blk_step * ppb == per_step and blk_step >= 1
    nblk = n_pages // ppb
    assert nblk % blk_step == 0
    seq_steps = nblk // blk_step
    kx, vt, km = pl.pallas_call(
        _moba_prep_body,
        grid=(nb,),
        in_specs=[
            pl.BlockSpec(memory_space=pltpu.SMEM),
            pl.BlockSpec((MOBA_BLOCK, width), lambda j: (j, 0)),
            pl.BlockSpec((MOBA_BLOCK, width), lambda j: (j, 0)),
        ],
        out_specs=[
            pl.BlockSpec((A_HEADS, 1, MOBA_BLOCK, 2 * LANES), lambda j: (0, j, 0, 0)),
            pl.BlockSpec((A_HEADS, 1, dh, MOBA_BLOCK), lambda j: (0, j, 0, 0)),
            pl.BlockSpec((A_HEADS, SEL_ROWS, dh), lambda j: (0, 0, 0)),
        ],
        out_shape=[
            jax.ShapeDtypeStruct((A_HEADS, nb, MOBA_BLOCK, 2 * LANES), BF16),
            jax.ShapeDtypeStruct((A_HEADS, nb, dh, MOBA_BLOCK), BF16),
            jax.ShapeDtypeStruct((A_HEADS, SEL_ROWS, dh), F32),
        ],
        compiler_params=_cparams(("arbitrary",)),
        name="moba_prep",
    )(slopes, k, v)
    return pl.pallas_call(
        functools.partial(_moba_prompt_body, n_pages=n_pages, pages_per_block=ppb),
        grid_spec=pltpu.PrefetchScalarGridSpec(
            num_scalar_prefetch=1,
            grid=(A_HEADS, nb),
            in_specs=[
                pl.BlockSpec(memory_space=pltpu.SMEM),
                pl.BlockSpec((MOBA_BLOCK, dh), lambda h, b, pt: (b, h)),
                pl.BlockSpec((1, nb, MOBA_BLOCK, 2 * LANES), lambda h, b, pt: (h, 0, 0, 0)),
                pl.BlockSpec((1, nb, dh, MOBA_BLOCK), lambda h, b, pt: (h, 0, 0, 0)),
                pl.BlockSpec((1, SEL_ROWS, dh), lambda h, b, pt: (h, 0, 0)),
                pl.BlockSpec(memory_space=pl.ANY),
            ],
            out_specs=[
                pl.BlockSpec((MOBA_BLOCK, dh), lambda h, b, pt: (b, h)),
                pl.BlockSpec((1, blk_step, A_HEADS, dh),
                             lambda h, b, pt: ((h * nb + b) // seq_steps, (h * nb + b) % seq_steps, 0, 0)),
            ],
            scratch_shapes=[
                pltpu.VMEM((nb, MOBA_BLOCK, MOBA_BLOCK), F32),
                pltpu.VMEM((2, per_step) + cache_k.shape[1:], F32),
                pltpu.SemaphoreType.DMA((2,)),
            ],
        ),
        out_shape=[
            jax.ShapeDtypeStruct((s_len, A_WIDTH), BF16),
            jax.ShapeDtypeStruct((bsz, nblk, A_HEADS, dh), F32),
        ],
        compiler_params=_cparams(("arbitrary", "arbitrary")),
        name="moba_prompt",
    )(page_table, slopes, q, kx, vt, km, cache_k)


def _sample_select_body(q_ref, kn_ref, ks_ref, o_ref, *, n_past_blocks):
    t = q_ref.shape[0]
    nbp = ks_ref.shape[1]
    lane = lax.broadcasted_iota(jnp.int32, (t, LANES), 1)
    first = lax.broadcasted_iota(jnp.int32, (8, A_HEAD_DIM), 0) == 0
    out = jnp.zeros((t, LANES), F32)
    for h in range(A_HEADS):
        hs = slice(h * A_HEAD_DIM, (h + 1) * A_HEAD_DIM)
        own = jnp.where(first, jnp.sum(kn_ref[:, hs], axis=0, keepdims=True), 0.0)
        ksum = jnp.concatenate([ks_ref[0, :, h, :], own, jnp.zeros((LANES - nbp - 8, A_HEAD_DIM), F32)], axis=0)
        s = lax.dot_general(q_ref[:, hs], ksum * (1.0 / MOBA_BLOCK), (((1,), (1,)), ((), ())),
                            precision=HIGHEST, preferred_element_type=F32)
        _, picks = _top3_mask(s, n_past_blocks, axis=1)
        for r, idx in enumerate(picks):
            out = jnp.where(lane == h * MOBA_TOPK + r, idx, out)
    o_ref[0] = out.astype(jnp.int32)


def _sample_select(q, k_new, ksum, t):
    bsz, nbp, heads, dh = ksum.shape
    width = heads * dh
    assert nbp % 8 == 0 and nbp + 8 <= LANES and t <= 8
    return pl.pallas_call(
        functools.partial(_sample_select_body, n_past_blocks=nbp),
        grid=(bsz,),
        in_specs=[
            pl.BlockSpec((t, width), lambda s: (s, 0)),
            pl.BlockSpec((t, width), lambda s: (s, 0)),
            pl.BlockSpec((1, nbp, heads, dh), lambda s: (s, 0, 0, 0)),
        ],
        out_specs=pl.BlockSpec((1, t, LANES), lambda s: (s, 0, 0)),
        out_shape=jax.ShapeDtypeStruct((bsz, t, LANES), jnp.int32),
        compiler_params=_cparams(("parallel",)),
        name="sample_select",
    )(q, k_new, ksum)


def _moba_sample_body(blk_ref, pt_ref, slope_ref, q_ref, kn_ref, vn_ref, ck_ref, cv_ref, o_ref,
                      kbuf, vbuf, sem, *, past, page):
    s = pl.program_id(0)
    h = pl.program_id(1)
    nh = pl.num_programs(1)
    step = s * nh + h
    slot = step % 2
    t = q_ref.shape[0]
    nsel = t * MOBA_TOPK
    ppb = MOBA_BLOCK // page

    def block_at(s_, h_, i):
        return blk_ref[s_, (i // MOBA_TOPK) * (A_HEADS * MOBA_TOPK) + h_ * MOBA_TOPK + i % MOBA_TOPK]

    def block_of(i):
        return block_at(s, h, i)

    def copies(s_, h_, slot_, i, c):
        phys = pt_ref[s_, block_at(s_, h_, i) * ppb + c]
        dst = pl.ds((i * ppb + c) * page, page)
        return (pltpu.make_async_copy(ck_ref.at[phys, :, h_, :], kbuf.at[slot_, dst, :], sem.at[slot_, 0]),
                pltpu.make_async_copy(cv_ref.at[phys, :, h_, :], vbuf.at[slot_, dst, :], sem.at[slot_, 1]))

    def start_all(s_, h_, slot_):
        for i in range(nsel):
            for c in range(ppb):
                ck, cv = copies(s_, h_, slot_, i, c)
                ck.start()
                cv.start()

    @pl.when(step == 0)
    def _():
        start_all(s, h, slot)

    nxt = step + 1

    @pl.when(nxt < pl.num_programs(0) * nh)
    def _():
        start_all(nxt // nh, nxt % nh, 1 - slot)

    slope = slope_ref[h]
    q = q_ref[...]
    qb = (q * (A_HEAD_DIM ** -0.5)).astype(BF16)
    ncol = nsel * MOBA_BLOCK
    row = lax.broadcasted_iota(jnp.int32, (t, ncol), 0)
    col = lax.broadcasted_iota(jnp.int32, (t, ncol), 1)
    blk_row = jnp.concatenate(
        [jnp.full((1, MOBA_BLOCK), block_of(i), jnp.int32) for i in range(nsel)], axis=1)
    kpos = blk_row * MOBA_BLOCK + (col & (MOBA_BLOCK - 1))
    qpos = past + row
    per_tok = MOBA_TOPK * MOBA_BLOCK
    mine = (col >= row * per_tok) & (col < (row + 1) * per_tok)
    bias = jnp.where(mine & (kpos <= qpos), -slope * (qpos - kpos).astype(F32), -jnp.inf)

    kn = kn_ref[...]
    r2 = lax.broadcasted_iota(jnp.int32, (t, t), 0)
    c2 = lax.broadcasted_iota(jnp.int32, (t, t), 1)
    s_own = lax.dot_general(qb, kn.astype(BF16), (((1,), (1,)), ((), ())), preferred_element_type=F32)
    s_own = jnp.where(c2 <= r2, s_own - slope * (r2 - c2).astype(F32), -jnp.inf)

    pltpu.make_async_copy(kbuf.at[slot], kbuf.at[slot], sem.at[slot, 0]).wait()
    pltpu.make_async_copy(vbuf.at[slot], vbuf.at[slot], sem.at[slot, 1]).wait()

    s_past = lax.dot_general(qb, kbuf[slot].astype(BF16), (((1,), (1,)), ((), ())),
                             preferred_element_type=F32) + bias
    m = jnp.maximum(jnp.max(s_past, axis=-1, keepdims=True), jnp.max(s_own, axis=-1, keepdims=True))
    p_past = jnp.exp(s_past - m)
    p_own = jnp.exp(s_own - m)
    l = jnp.sum(p_past, axis=-1, keepdims=True) + jnp.sum(p_own, axis=-1, keepdims=True)
    acc = jnp.dot(p_past.astype(BF16), vbuf[slot].astype(BF16), preferred_element_type=F32)
    acc = acc + jnp.dot(p_own.astype(BF16), vn_ref[...].astype(BF16), preferred_element_type=F32)
    o_ref[...] = acc / l


def _moba_sample(q, k_new, v_new, blocks, page_table, cache_k, cache_v, slopes, t):
    bsz, n_pages = page_table.shape
    page = cache_k.shape[1]
    past = n_pages * page
    assert past % MOBA_BLOCK == 0 and MOBA_BLOCK % page == 0 and t <= MOBA_BLOCK
    dh = A_HEAD_DIM
    rows = t * MOBA_TOPK * MOBA_BLOCK
    tok = lambda s, h, *_: (s, h)
    return pl.pallas_call(
        functools.partial(_moba_sample_body, past=past, page=page),
        grid_spec=pltpu.PrefetchScalarGridSpec(
            num_scalar_prefetch=2,
            grid=(bsz, A_HEADS),
            in_specs=[
                pl.BlockSpec(memory_space=pltpu.SMEM),
                pl.BlockSpec((t, dh), tok),
                pl.BlockSpec((t, dh), tok),
                pl.BlockSpec((t, dh), tok),
                pl.BlockSpec(memory_space=pl.ANY),
                pl.BlockSpec(memory_space=pl.ANY),
            ],
            out_specs=pl.BlockSpec((t, dh), tok),
            scratch_shapes=[
                pltpu.VMEM((2, rows, dh), F32),
                pltpu.VMEM((2, rows, dh), F32),
                pltpu.SemaphoreType.DMA((2, 2)),
            ],
        ),
        out_shape=jax.ShapeDtypeStruct((bsz * t, A_WIDTH), F32),
        compiler_params=_cparams(("arbitrary", "arbitrary")),
        name="moba_sample",
    )(blocks[:, :, :A_HEADS * MOBA_TOPK].reshape(bsz, -1), page_table, slopes, q, k_new, v_new,
      cache_k, cache_v)


def _gla_body(q_ref, k_ref, v_ref, r_ref, z_ref, w2_ref, bg_ref, go_ref, s0_ref,
              o_ref, sf_ref, st_ref, *, t_blk):
    c = pl.program_id(1)
    dk, dv = B_KEY_DIM, B_VAL_DIM

    @pl.when(c == 0)
    def _():
        for h in range(B_HEADS):
            st_ref[h] = s0_ref[0, h].T

    for h in range(B_HEADS):
        ks = slice(h * dk, (h + 1) * dk)
        vs = slice(h * dv, (h + 1) * dv)
        on, st_new = _gla_head(q_ref[:, ks], k_ref[:, ks], v_ref[:, vs], z_ref[...], r_ref[:, vs],
                               w2_ref[:, ks], bg_ref[:, ks], go_ref[...], st_ref[h], t_blk)
        st_ref[h] = st_new
        o_ref[:, vs] = on.astype(o_ref.dtype)

    @pl.when(c == pl.num_programs(1) - 1)
    def _():
        for h in range(B_HEADS):
            sf_ref[0, h] = st_ref[h].T


def _gla_head(q, k, v, z, rb, w2, bg, go, st, t_blk):
    cs = GLA_CHUNK
    dk = B_KEY_DIM

    def rows(x):
        if t_blk < cs:
            x = jnp.concatenate([x, jnp.zeros((cs - t_blk, x.shape[1]), x.dtype)], axis=0)
        return x

    q = rows(q) * (dk ** -0.5)
    k = rows(k)
    v = rows(v)
    pre = jnp.dot(rows(z), w2, precision=HIGHEST, preferred_element_type=F32) + bg
    la = (jnp.minimum(pre, 0.0) - jnp.log1p(jnp.exp(-jnp.abs(pre)))) * (1.0 / GATE_TEMP)
    ri = lax.broadcasted_iota(jnp.int32, (cs, cs), 0)
    ci = lax.broadcasted_iota(jnp.int32, (cs, cs), 1)
    if t_blk < cs:
        la = jnp.where(lax.broadcasted_iota(jnp.int32, la.shape, 0) < t_blk, la, 0.0)
    g = jnp.dot((ci <= ri).astype(F32), la, precision=HIGHEST, preferred_element_type=F32)
    g_last = g[cs - 1:cs, :]

    nsub = cs // GLA_SUB
    refs = [jnp.zeros((GLA_SUB, dk), F32)]
    for i in range(1, nsub):
        refs.append(jnp.broadcast_to(g[i * GLA_SUB - 1:i * GLA_SUB, :], (GLA_SUB, dk)))
    a = (q * jnp.exp(g - jnp.concatenate(refs, axis=0))).astype(BF16)

    far = [jnp.zeros((GLA_SUB, cs), F32)]
    for i in range(1, nsub):
        r_i = g[i * GLA_SUB - 1:i * GLA_SUB, :]
        b_i = (k * jnp.exp(jnp.minimum(r_i - g, 0.0))).astype(BF16)
        far.append(lax.dot_general(a[i * GLA_SUB:(i + 1) * GLA_SUB], b_i, (((1,), (1,)), ((), ())),
                                   preferred_element_type=F32))
    att = jnp.where((ri // GLA_SUB) > (ci // GLA_SUB), jnp.concatenate(far, axis=0), 0.0)
    for d in range(GLA_SUB):
        if d == 0:
            w = q * k
        else:
            e = jnp.exp(jnp.minimum(g - pltpu.roll(g, d, 0), 0.0))
            w = q * pltpu.roll(k, d, 0) * e
        a_d = jnp.sum(w, axis=-1, keepdims=True)
        att = jnp.where((ri - ci == d) & (ri % GLA_SUB >= d), a_d, att)

    o = lax.dot_general((q * jnp.exp(g)).astype(BF16), st.astype(BF16), (((1,), (1,)), ((), ())),
                        preferred_element_type=F32)
    vb = v.astype(BF16)
    o = o + jnp.dot(att.astype(BF16), vb, preferred_element_type=F32)
    kd = (k * jnp.exp(g_last - g)).astype(BF16)
    st_new = jnp.exp(g_last) * st + lax.dot_general(vb, kd, (((0,), (0,)), ((), ())),
                                                    preferred_element_type=F32)
    on = _rmsnorm(o[:t_blk], go) * (rb * jax.nn.sigmoid(rb))
    return on, st_new


def _gla(proj, w2p, bg, g_out, s0, t, out_dtype):
    bsz = s0.shape[0]
    t_blk = min(t, GLA_CHUNK)
    assert t % t_blk == 0
    nc = t // t_blk
    dk, dv = B_KEY_DIM, B_VAL_DIM
    assert BV_WIDTH == 2 * BK_WIDTH
    rowblk = lambda b, c: b * nc + c
    o, sf = pl.pallas_call(
        functools.partial(_gla_body, t_blk=t_blk),
        grid=(bsz, nc),
        in_specs=[
            pl.BlockSpec((t_blk, BK_WIDTH), lambda b, c: (rowblk(b, c), 0)),
            pl.BlockSpec((t_blk, BK_WIDTH), lambda b, c: (rowblk(b, c), 1)),
            pl.BlockSpec((t_blk, BV_WIDTH), lambda b, c: (rowblk(b, c), 1)),
            pl.BlockSpec((t_blk, BV_WIDTH), lambda b, c: (rowblk(b, c), 2)),
            pl.BlockSpec((t_blk, LANES), lambda b, c: (rowblk(b, c), (2 * BK_WIDTH + 2 * BV_WIDTH) // LANES)),
            pl.BlockSpec((LANES, BK_WIDTH), lambda b, c: (0, 0)),
            pl.BlockSpec((1, BK_WIDTH), lambda b, c: (0, 0)),
            pl.BlockSpec((1, dv), lambda b, c: (0, 0)),
            pl.BlockSpec((1, B_HEADS, dk, dv), lambda b, c: (b, 0, 0, 0)),
        ],
        out_specs=[
            pl.BlockSpec((t_blk, BV_WIDTH), lambda b, c: (rowblk(b, c), 0)),
            pl.BlockSpec((1, B_HEADS, dk, dv), lambda b, c: (b, 0, 0, 0)),
        ],
        out_shape=[
            jax.ShapeDtypeStruct((bsz * t, BV_WIDTH), out_dtype),
            jax.ShapeDtypeStruct(s0.shape, s0.dtype),
        ],
        scratch_shapes=[pltpu.VMEM((B_HEADS, dv, dk), F32)],
        compiler_params=_cparams(("parallel", "arbitrary")),
        name="gla",
    )(proj, proj, proj, proj, proj, w2p, bg, g_out, s0)
    return o, sf


def _layer(x, pe, s0, moba_fn, t, wts, mix_dtype):
    h1, u = _ffn(x, wts["g_ffn1"], wts["w1g"], wts["w1u"], wts["w1d"], wts["g_mix"])
    qa = _mm(u, wts["w_qa"], name="proj_qa")
    ka = _mm(u, wts["w_ka"], name="proj_ka")
    va = _mm(u, wts["w_va"], name="proj_va")
    gl = _mm(u, wts["w_gla"], name="proj_gla")
    oa = moba_fn(qa, ka, va)
    ob, s_fin = _gla(gl, wts["w_gate_b2"], wts["b_gate_b"], wts["g_gla_out"], s0, t, mix_dtype)
    mix = _merge(u, oa, ob, wts["w_ga"], wts["w_gb"], wts["w_proj_a"], wts["w_proj_b"])
    h2 = _mm(mix, wts["w_out"], residual=h1, name="out_proj")
    h3, w = _ffn(h2, wts["g_ffn2"], wts["w2g"], wts["w2u"], wts["w2d"], wts["g_ple"])
    y = _ple_final(w, h3, pe, wts["w_ple_gate"], wts["w_ple_proj"], wts["g_final"])
    return y, ka, va, s_fin


def kernel(x_prompt, x_sample, cache_k, cache_v, state_gla, page_table, p_prompt, p_sample, g_ffn1, w_ffn1_gate, w_ffn1_up, w_ffn1_down, g_mix, w_in, w_gate_b2, b_gate_b, g_gla_out, w_proj_a, w_proj_b, w_out, g_ffn2, w_ffn2_gate, w_ffn2_up, w_ffn2_down, g_ple, w_ple_gate, w_ple_proj, g_final):
    depth = w_in.shape[0]
    assert depth == 1, "single trunk layer"
    li = 0
    bp, sp, d = x_prompt.shape
    bs, ts, _ = x_sample.shape
    assert bp == 1
    bf = lambda a: a.astype(BF16)
    row = lambda a: a.reshape(1, -1)

    o_qa, o_ka, o_va = 0, A_WIDTH, 2 * A_WIDTH
    o_gla = 3 * A_WIDTH
    o_zg = o_gla + 2 * BK_WIDTH + 2 * BV_WIDTH
    o_ga = o_zg + GATE_RANK
    o_gb = o_ga + d
    wi = w_in[li]
    w_zg = jnp.pad(wi[:, o_zg:o_ga], ((0, 0), (0, LANES - GATE_RANK)))
    wts = dict(
        g_ffn1=row(g_ffn1[li]), w1g=bf(w_ffn1_gate[li]), w1u=bf(w_ffn1_up[li]), w1d=bf(w_ffn1_down[li]),
        g_mix=row(g_mix[li]),
        w_qa=bf(wi[:, o_qa:o_ka]), w_ka=bf(wi[:, o_ka:o_va]), w_va=bf(wi[:, o_va:o_gla]),
        w_gla=bf(jnp.concatenate([wi[:, o_gla:o_zg], w_zg], axis=1)),
        w_ga=bf(wi[:, o_ga:o_gb]), w_gb=bf(wi[:, o_gb:]),
        w_gate_b2=jnp.pad(w_gate_b2[li], ((0, LANES - GATE_RANK), (0, 0))),
        b_gate_b=row(b_gate_b[li]), g_gla_out=row(g_gla_out[li]),
        w_proj_a=bf(w_proj_a[li]), w_proj_b=bf(w_proj_b[li]), w_out=bf(w_out[li]),
        g_ffn2=row(g_ffn2[li]), w2g=bf(w_ffn2_gate[li]), w2u=bf(w_ffn2_up[li]), w2d=bf(w_ffn2_down[li]),
        g_ple=row(g_ple[li]), w_ple_gate=bf(w_ple_gate[li]), w_ple_proj=bf(w_ple_proj[li]),
        g_final=row(g_final),
    )
    slopes = jnp.exp2(-8.0 * jnp.arange(1, A_HEADS + 1, dtype=F32) / A_HEADS)

    ck = cache_k.reshape(cache_k.shape[1:])
    cv = cache_v.reshape(cache_v.shape[1:])

    s0_prompt = jnp.zeros((bp, B_HEADS, B_KEY_DIM, B_VAL_DIM), state_gla.dtype)
    cached_key_sums = []

    def moba_p(q, k, v):
        oa, ksum = _moba_prompt(q, k, v, slopes, ck, page_table)
        cached_key_sums.append(ksum)
        return oa

    yp, kp, vp, stp = _layer(x_prompt.reshape(sp, d), p_prompt[li].reshape(sp, -1), s0_prompt,
                             moba_p, sp, wts, BF16)

    def moba_s(q, k, v):
        blocks = _sample_select(q, k, cached_key_sums[0], ts)
        return _moba_sample(q, k, v, blocks, page_table, ck, cv, slopes, ts)

    ys, ks, vs, sts = _layer(x_sample.reshape(bs * ts, d), p_sample[li].reshape(bs * ts, -1), state_gla[li],
                             moba_s, ts, wts, F32)

    heads = (A_HEADS, A_HEAD_DIM)
    return (yp.reshape(bp, sp, d), ys.reshape(bs, ts, d),
            kp.reshape(1, bp, sp, *heads), vp.reshape(1, bp, sp, *heads),
            ks.reshape(1, bs, ts, *heads), vs.reshape(1, bs, ts, *heads),
            stp[None], sts[None])
```

```python
import functools

import jax
import jax.numpy as jnp
from jax import lax
from jax.experimental import pallas as pl
from jax.experimental.pallas import tpu as pltpu

F32 = jnp.float32
BF16 = jnp.bfloat16
HIGHEST = lax.Precision.HIGHEST

EPS = 1e-6
A_HEADS = 16
A_HEAD_DIM = 128
A_WIDTH = A_HEADS * A_HEAD_DIM
MOBA_BLOCK = 256
MOBA_TOPK = 3
B_HEADS = 4
B_KEY_DIM = 256
B_VAL_DIM = 512
BK_WIDTH = B_HEADS * B_KEY_DIM
BV_WIDTH = B_HEADS * B_VAL_DIM
GATE_RANK = 16
GATE_TEMP = 16.0
GLA_CHUNK = 64
GLA_SUB = 16

LANES = 128
VMEM_LIMIT = 52 * 1024 * 1024
MASK_BIG = float(2 ** 30)

LOG2E = 1.4426950408889634
MOBA_GROUP = 4

SEL_ROWS = 32
VT_EXTRA_ROWS = 16
COL_KPOS = SEL_ROWS
COL_QPOS = COL_KPOS + 3


def _cparams(sem):
    return pltpu.CompilerParams(dimension_semantics=sem, vmem_limit_bytes=VMEM_LIMIT)


def _rmsnorm(x, g):
    return x * lax.rsqrt(jnp.mean(x * x, axis=-1, keepdims=True) + EPS) * g


def _tile(n, cap):
    if n <= cap:
        return n
    t = (cap // LANES) * LANES
    while n % t:
        t -= LANES
    return t


def _ffn_body(x_ref, g_ref, wg_ref, wu_ref, wd_ref, gn_ref, h_ref, u_ref, xn_ref):
    j = pl.program_id(1)

    @pl.when(j == 0)
    def _():
        xn_ref[...] = _rmsnorm(x_ref[...], g_ref[...]).astype(BF16)
        h_ref[...] = jnp.zeros_like(h_ref)

    xn = xn_ref[...]
    a = jnp.dot(xn, wg_ref[...], preferred_element_type=F32)
    b = jnp.dot(xn, wu_ref[...], preferred_element_type=F32)
    mid = (a * jax.nn.sigmoid(a) * b).astype(BF16)
    h_ref[...] += jnp.dot(mid, wd_ref[...], preferred_element_type=F32)

    @pl.when(j == pl.num_programs(1) - 1)
    def _():
        h = x_ref[...] + 0.5 * h_ref[...]
        h_ref[...] = h
        u_ref[...] = _rmsnorm(h, gn_ref[...]).astype(BF16)


def _ffn(x, g, wg, wu, wd, g_next):
    m, d = x.shape
    f = wg.shape[1]
    tm = _tile(m, 512)
    tf = _tile(f, 512)
    return pl.pallas_call(
        _ffn_body,
        grid=(m // tm, f // tf),
        in_specs=[
            pl.BlockSpec((tm, d), lambda i, j: (i, 0)),
            pl.BlockSpec((1, d), lambda i, j: (0, 0)),
            pl.BlockSpec((d, tf), lambda i, j: (0, j)),
            pl.BlockSpec((d, tf), lambda i, j: (0, j)),
            pl.BlockSpec((tf, d), lambda i, j: (j, 0)),
            pl.BlockSpec((1, d), lambda i, j: (0, 0)),
        ],
        out_specs=[
            pl.BlockSpec((tm, d), lambda i, j: (i, 0)),
            pl.BlockSpec((tm, d), lambda i, j: (i, 0)),
        ],
        out_shape=[jax.ShapeDtypeStruct((m, d), F32), jax.ShapeDtypeStruct((m, d), BF16)],
        scratch_shapes=[pltpu.VMEM((tm, d), BF16)],
        compiler_params=_cparams(("parallel", "arbitrary")),
        name="ffn",
    )(x, g, wg, wu, wd, g_next)


def _mm_body(x_ref, w_ref, o_ref):
    o_ref[...] = jnp.dot(x_ref[...], w_ref[...], preferred_element_type=F32).astype(o_ref.dtype)


def _mm_res_body(x_ref, w_ref, r_ref, o_ref):
    acc = jnp.dot(x_ref[...], w_ref[...], preferred_element_type=F32)
    o_ref[...] = (acc + r_ref[...]).astype(o_ref.dtype)


def _mm(x, w, out_dtype=F32, residual=None, name="mm"):
    m, k = x.shape
    n = w.shape[1]
    tm = _tile(m, 1024)
    tn = _tile(n, 1024)
    in_specs = [
        pl.BlockSpec((tm, k), lambda j, i: (i, 0)),
        pl.BlockSpec((k, tn), lambda j, i: (0, j)),
    ]
    args = [x, w]
    body = _mm_body
    if residual is not None:
        in_specs.append(pl.BlockSpec((tm, tn), lambda j, i: (i, j)))
        args.append(residual)
        body = _mm_res_body
    return pl.pallas_call(
        body,
        grid=(n // tn, m // tm),
        in_specs=in_specs,
        out_specs=pl.BlockSpec((tm, tn), lambda j, i: (i, j)),
        out_shape=jax.ShapeDtypeStruct((m, n), out_dtype),
        compiler_params=_cparams(("parallel", "parallel")),
        name=name,
    )(*args)


def _merge_body(u_ref, oa_ref, ob_ref, wga_ref, wgb_ref, wpa_ref, wpb_ref, o_ref):
    u = u_ref[...]
    ga = jnp.dot(u, wga_ref[...], preferred_element_type=F32)
    gb = jnp.dot(u, wgb_ref[...], preferred_element_type=F32)
    pa = jnp.dot(oa_ref[...].astype(BF16), wpa_ref[...], preferred_element_type=F32)
    pb = jnp.dot(ob_ref[...].astype(BF16), wpb_ref[...], preferred_element_type=F32)
    o_ref[...] = (jax.nn.sigmoid(ga) * pa + jax.nn.sigmoid(gb) * pb).astype(o_ref.dtype)


def _merge(u, oa, ob, wga, wgb, wpa, wpb):
    m, d = u.shape
    n = wga.shape[1]
    tm = _tile(m, 512)
    tn = _tile(n, 512)
    row = lambda i, j: (i, 0)
    col = lambda i, j: (0, j)
    return pl.pallas_call(
        _merge_body,
        grid=(m // tm, n // tn),
        in_specs=[
            pl.BlockSpec((tm, d), row),
            pl.BlockSpec((tm, oa.shape[1]), row),
            pl.BlockSpec((tm, ob.shape[1]), row),
            pl.BlockSpec((d, tn), col),
            pl.BlockSpec((d, tn), col),
            pl.BlockSpec((oa.shape[1], tn), col),
            pl.BlockSpec((ob.shape[1], tn), col),
        ],
        out_specs=pl.BlockSpec((tm, tn), lambda i, j: (i, j)),
        out_shape=jax.ShapeDtypeStruct((m, n), BF16),
        compiler_params=_cparams(("parallel", "parallel")),
        name="merge",
    )(u, oa, ob, wga, wgb, wpa, wpb)


def _ple_body(w_ref, h_ref, pe_ref, wpg_ref, wpp_ref, gf_ref, y_ref):
    gate = jax.nn.sigmoid(jnp.dot(w_ref[...], wpg_ref[...], preferred_element_type=F32))
    pp = jnp.dot(pe_ref[...].astype(BF16), wpp_ref[...], preferred_element_type=F32)
    y_ref[...] = _rmsnorm(h_ref[...] + gate * pp, gf_ref[...])


def _ple_final(w, h, pe, wpg, wpp, g_final):
    m, d = h.shape
    tm = _tile(m, 256)
    row = lambda i: (i, 0)
    whole = lambda i: (0, 0)
    return pl.pallas_call(
        _ple_body,
        grid=(m // tm,),
        in_specs=[
            pl.BlockSpec((tm, d), row),
            pl.BlockSpec((tm, d), row),
            pl.BlockSpec((tm, pe.shape[1]), row),
            pl.BlockSpec(wpg.shape, whole),
            pl.BlockSpec(wpp.shape, whole),
            pl.BlockSpec((1, d), whole),
        ],
        out_specs=pl.BlockSpec((tm, d), row),
        out_shape=jax.ShapeDtypeStruct((m, d), F32),
        compiler_params=_cparams(("parallel",)),
        name="ple_final",
    )(w, h, pe, wpg, wpp, g_final)


def _top3_mask(s, n_valid, axis):
    nb = s.shape[axis]
    blk = lax.broadcasted_iota(jnp.int32, s.shape, axis)
    lane_f = blk.astype(F32)
    neg = jnp.float32(-jnp.inf)
    s = jnp.where(blk < n_valid, s, neg)
    sel = jnp.zeros(s.shape, F32)
    picks = []
    for _ in range(MOBA_TOPK):
        best = jnp.max(s, axis=axis, keepdims=True)
        idx = jnp.min(jnp.where(s == best, lane_f, float(nb)), axis=axis, keepdims=True)
        hit = (lane_f == idx) & (best > neg)
        sel = jnp.where(hit, 1.0, sel)
        s = jnp.where(hit, neg, s)
        picks.append(idx)
    return sel, picks


def _split3(x):
    hi = x.astype(BF16).astype(F32)
    mid = (x - hi).astype(BF16).astype(F32)
    lo = (x - hi - mid).astype(BF16).astype(F32)
    return hi, mid, lo


def _dot_bf16x3(a, b):
    a_hi = a.astype(BF16)
    a_lo = (a - a_hi.astype(F32)).astype(BF16)
    b_hi = b.astype(BF16)
    b_lo = (b - b_hi.astype(F32)).astype(BF16)
    return jnp.dot(jnp.concatenate([a_hi, a_hi, a_lo], axis=1), jnp.concatenate([b_hi, b_lo, b_hi], axis=0),
                   preferred_element_type=F32)


def _moba_prep_body(slope_ref, q_ref, k_ref, v_ref, qx_ref, kx_ref, vt_ref, km_ref):
    j = pl.program_id(0)
    nb = pl.num_programs(0)
    blk = MOBA_BLOCK

    @pl.when(j == 0)
    def _():
        km_ref[...] = jnp.zeros_like(km_ref)

    lane = lax.broadcasted_iota(jnp.int32, (blk, LANES), 1)
    kpos = (j * blk + lax.broadcasted_iota(jnp.int32, (blk, LANES), 0)).astype(F32)
    row_sel = lax.broadcasted_iota(jnp.int32, (SEL_ROWS, blk), 0)
    row_ext = lax.broadcasted_iota(jnp.int32, (LANES - SEL_ROWS, blk), 0) + SEL_ROWS
    qpos = (j * blk + lax.broadcasted_iota(jnp.int32, (LANES - SEL_ROWS, blk), 1)).astype(F32)
    ones_row = jnp.where(lax.broadcasted_iota(jnp.int32, (VT_EXTRA_ROWS, blk), 0) == 0, 1.0, 0.0)
    for h in range(A_HEADS):
        hs = slice(h * A_HEAD_DIM, (h + 1) * A_HEAD_DIM)
        qt = q_ref[:, hs].T
        sel, _ = _top3_mask(_dot_bf16x3(km_ref[h], qt), j, axis=0)
        pen = jnp.where((row_sel == j) | (row_sel >= nb), 0.0, (sel - 1.0) * MASK_BIG)
        c_hi, c_mid, c_lo = _split3((slope_ref[h] * -LOG2E) * qpos)
        ext = jnp.where((row_ext >= COL_KPOS) & (row_ext < COL_KPOS + 3), 1.0, 0.0)
        ext = jnp.where(row_ext == COL_QPOS, c_hi, ext)
        ext = jnp.where(row_ext == COL_QPOS + 1, c_mid, ext)
        ext = jnp.where(row_ext == COL_QPOS + 2, c_lo, ext)
        qx_ref[h, 0] = jnp.concatenate([(qt * (A_HEAD_DIM ** -0.5 * LOG2E)).astype(BF16), pen.astype(BF16),
                                        ext.astype(BF16)], axis=0)
        k = k_ref[:, hs]
        km_ref[h, pl.ds(j, 1), :] = jnp.sum(k, axis=0, keepdims=True) * (1.0 / MOBA_BLOCK)
        vt_ref[h, 0] = jnp.concatenate([v_ref[:, hs].T, ones_row], axis=0).astype(BF16)
        a_hi, a_mid, a_lo = _split3((slope_ref[h] * LOG2E) * kpos)
        ext = jnp.where(lane == j, 1.0, 0.0)
        ext = jnp.where(lane == COL_KPOS, a_hi, ext)
        ext = jnp.where(lane == COL_KPOS + 1, a_mid, ext)
        ext = jnp.where(lane == COL_KPOS + 2, a_lo, ext)
        ext = jnp.where((lane >= COL_QPOS) & (lane < COL_QPOS + 3), 1.0, ext)
        kx_ref[h, 0] = jnp.concatenate([k.astype(BF16), ext.astype(BF16)], axis=1)


def _moba_prompt_body(pt_ref, qx_ref, kx_ref, vt_ref, ck_ref, o_ref, ks_ref,
                      s_buf, page_buf, sem, *, first_block, n_pages, pages_per_block):
    h = pl.program_id(0)
    n_tiles = kx_ref.shape[1]
    b = first_block + pl.program_id(1)
    blk = MOBA_BLOCK
    step = h * pl.num_programs(1) + pl.program_id(1)
    slot = step % 2
    per_step = page_buf.shape[1]

    seq_steps = n_pages // per_step

    def page_copies(step_, slot_):
        seq = step_ // seq_steps
        first = (step_ - seq * seq_steps) * per_step
        return [pltpu.make_async_copy(ck_ref.at[pt_ref[seq, first + c]], page_buf.at[slot_, c], sem.at[slot_])
                for c in range(per_step)]

    @pl.when(step == 0)
    def _():
        for cp in page_copies(step, slot):
            cp.start()

    @pl.when(step + 1 < pl.num_programs(0) * pl.num_programs(1))
    def _():
        for cp in page_copies(step + 1, 1 - slot):
            cp.start()

    qx = qx_ref[0, 0]
    kr = lax.broadcasted_iota(jnp.int32, (blk, blk), 0)
    qc = lax.broadcasted_iota(jnp.int32, (blk, blk), 1)

    m = jnp.full((1, blk), -MASK_BIG, F32)
    for j in range(n_tiles):
        s = jnp.dot(kx_ref[0, j], qx, preferred_element_type=F32)
        if j >= first_block:
            s = jnp.where(j * blk + kr <= b * blk + qc, s, -MASK_BIG)
        s_buf[j] = s
        m = jnp.maximum(m, jnp.max(s, axis=0, keepdims=True))

    acc = jnp.zeros((A_HEAD_DIM + VT_EXTRA_ROWS, blk), F32)
    for j in range(n_tiles):
        p = jnp.exp2((s_buf[j] - m).astype(BF16))
        acc = acc + jnp.dot(vt_ref[0, j], p, preferred_element_type=F32)
    o_ref[...] = (acc[:A_HEAD_DIM] / acc[A_HEAD_DIM:A_HEAD_DIM + 1]).T.astype(o_ref.dtype)

    pltpu.make_async_copy(page_buf.at[slot], page_buf.at[slot], sem.at[slot]).wait()
    for i in range(per_step // pages_per_block):
        total = jnp.sum(page_buf[slot, i * pages_per_block], axis=0)
        for c in range(1, pages_per_block):
            total = total + jnp.sum(page_buf[slot, i * pages_per_block + c], axis=0)
        ks_ref[0, i] = total


def _moba_prompt(q, k, v, slopes, cache_k, page_table):
    s_len = q.shape[0]
    nb = s_len // MOBA_BLOCK
    assert nb * MOBA_BLOCK == s_len and nb % MOBA_GROUP == 0 and MOBA_TOPK <= nb <= SEL_ROWS
    dh = A_HEAD_DIM
    width = A_HEADS * dh
    bsz, n_pages = page_table.shape
    page = cache_k.shape[1]
    ppb = MOBA_BLOCK // page
    ncalls = nb // MOBA_GROUP
    seqs_call = bsz // ncalls
    nsteps = A_HEADS * MOBA_GROUP
    per_step = seqs_call * n_pages // nsteps
    blk_step = per_step // ppb
    assert seqs_call * ncalls == bsz and per_step * nsteps == seqs_call * n_pages
    assert blk_step * ppb == per_step and blk_step >= 1
    nblk = n_pages // ppb
    assert nblk % blk_step == 0
    seq_steps = nblk // blk_step
    vrows = dh + VT_EXTRA_ROWS
    row_blk = pl.BlockSpec((MOBA_BLOCK, width), lambda j: (j, 0))
    qx, kx, vt = pl.pallas_call(
        _moba_prep_body,
        grid=(nb,),
        in_specs=[pl.BlockSpec(memory_space=pltpu.SMEM), row_blk, row_blk, row_blk],
        out_specs=[
            pl.BlockSpec((A_HEADS, 1, 2 * LANES, MOBA_BLOCK), lambda j: (0, j, 0, 0)),
            pl.BlockSpec((A_HEADS, 1, MOBA_BLOCK, 2 * LANES), lambda j: (0, j, 0, 0)),
            pl.BlockSpec((A_HEADS, 1, vrows, MOBA_BLOCK), lambda j: (0, j, 0, 0)),
        ],
        out_shape=[
            jax.ShapeDtypeStruct((A_HEADS, nb, 2 * LANES, MOBA_BLOCK), BF16),
            jax.ShapeDtypeStruct((A_HEADS, nb, MOBA_BLOCK, 2 * LANES), BF16),
            jax.ShapeDtypeStruct((A_HEADS, nb, vrows, MOBA_BLOCK), BF16),
        ],
        scratch_shapes=[pltpu.VMEM((A_HEADS, SEL_ROWS, dh), F32)],
        compiler_params=_cparams(("arbitrary",)),
        name="moba_prep",
    )(slopes, q, k, v)

    def attend(c):
        first = c * MOBA_GROUP
        n_tiles = first + MOBA_GROUP
        return pl.pallas_call(
            functools.partial(_moba_prompt_body, first_block=first, n_pages=n_pages, pages_per_block=ppb),
            grid_spec=pltpu.PrefetchScalarGridSpec(
                num_scalar_prefetch=1,
                grid=(A_HEADS, MOBA_GROUP),
                in_specs=[
                    pl.BlockSpec((1, 1, 2 * LANES, MOBA_BLOCK), lambda h, i, pt: (h, first + i, 0, 0)),
                    pl.BlockSpec((1, n_tiles, MOBA_BLOCK, 2 * LANES), lambda h, i, pt: (h, 0, 0, 0)),
                    pl.BlockSpec((1, n_tiles, vrows, MOBA_BLOCK), lambda h, i, pt: (h, 0, 0, 0)),
                    pl.BlockSpec(memory_space=pl.ANY),
                ],
                out_specs=[
                    pl.BlockSpec((MOBA_BLOCK, dh), lambda h, i, pt: (i, h)),
                    pl.BlockSpec((1, blk_step, A_HEADS, dh),
                                 lambda h, i, pt: ((h * MOBA_GROUP + i) // seq_steps,
                                                   (h * MOBA_GROUP + i) % seq_steps, 0, 0)),
                ],
                scratch_shapes=[
                    pltpu.VMEM((n_tiles, MOBA_BLOCK, MOBA_BLOCK), F32),
                    pltpu.VMEM((2, per_step) + cache_k.shape[1:], F32),
                    pltpu.SemaphoreType.DMA((2,)),
                ],
            ),
            out_shape=[
                jax.ShapeDtypeStruct((MOBA_GROUP * MOBA_BLOCK, A_WIDTH), BF16),
                jax.ShapeDtypeStruct((seqs_call, nblk, A_HEADS, dh), F32),
            ],
            compiler_params=_cparams(("arbitrary", "arbitrary")),
            name=f"moba_prompt_{c}",
        )(page_table[c * seqs_call:(c + 1) * seqs_call], qx, kx, vt, cache_k)

    parts = [attend(c) for c in range(ncalls)]
    return (jnp.concatenate([o for o, _ in parts], axis=0),
            jnp.concatenate([ks for _, ks in parts], axis=0))


def _sample_select_body(q_ref, kn_ref, ks_ref, o_ref, *, n_past_blocks):
    t = q_ref.shape[0]
    nbp = ks_ref.shape[1]
    lane = lax.broadcasted_iota(jnp.int32, (t, LANES), 1)
    first = lax.broadcasted_iota(jnp.int32, (8, A_HEAD_DIM), 0) == 0
    out = jnp.zeros((t, LANES), F32)
    for h in range(A_HEADS):
        hs = slice(h * A_HEAD_DIM, (h + 1) * A_HEAD_DIM)
        own = jnp.where(first, jnp.sum(kn_ref[:, hs], axis=0, keepdims=True), 0.0)
        ksum = jnp.concatenate([ks_ref[0, :, h, :], own, jnp.zeros((LANES - nbp - 8, A_HEAD_DIM), F32)], axis=0)
        s = lax.dot_general(q_ref[:, hs], ksum * (1.0 / MOBA_BLOCK), (((1,), (1,)), ((), ())),
                            precision=HIGHEST, preferred_element_type=F32)
        _, picks = _top3_mask(s, n_past_blocks, axis=1)
        for r, idx in enumerate(picks):
            out = jnp.where(lane == h * MOBA_TOPK + r, idx, out)
    o_ref[0] = out.astype(jnp.int32)


def _sample_select(q, k_new, ksum, t):
    bsz, nbp, heads, dh = ksum.shape
    width = heads * dh
    assert nbp % 8 == 0 and nbp + 8 <= LANES and t <= 8
    return pl.pallas_call(
        functools.partial(_sample_select_body, n_past_blocks=nbp),
        grid=(bsz,),
        in_specs=[
            pl.BlockSpec((t, width), lambda s: (s, 0)),
            pl.BlockSpec((t, width), lambda s: (s, 0)),
            pl.BlockSpec((1, nbp, heads, dh), lambda s: (s, 0, 0, 0)),
        ],
        out_specs=pl.BlockSpec((1, t, LANES), lambda s: (s, 0, 0)),
        out_shape=jax.ShapeDtypeStruct((bsz, t, LANES), jnp.int32),
        compiler_params=_cparams(("parallel",)),
        name="sample_select",
    )(q, k_new, ksum)


def _moba_sample_body(blk_ref, pt_ref, slope_ref, q_ref, kn_ref, vn_ref, ck_ref, cv_ref, o_ref,
                      kbuf, vbuf, sem, *, past, page):
    s = pl.program_id(0)
    h = pl.program_id(1)
    nh = pl.num_programs(1)
    step = s * nh + h
    slot = step % 2
    t = q_ref.shape[0]
    nsel = t * MOBA_TOPK
    ppb = MOBA_BLOCK // page

    def block_at(s_, h_, i):
        return blk_ref[s_, (i // MOBA_TOPK) * (A_HEADS * MOBA_TOPK) + h_ * MOBA_TOPK + i % MOBA_TOPK]

    def block_of(i):
        return block_at(s, h, i)

    def copies(s_, h_, slot_, i, c):
        phys = pt_ref[s_, block_at(s_, h_, i) * ppb + c]
        dst = pl.ds((i * ppb + c) * page, page)
        return (pltpu.make_async_copy(ck_ref.at[phys, :, h_, :], kbuf.at[slot_, dst, :], sem.at[slot_, 0]),
                pltpu.make_async_copy(cv_ref.at[phys, :, h_, :], vbuf.at[slot_, dst, :], sem.at[slot_, 1]))

    def start_all(s_, h_, slot_):
        for i in range(nsel):
            for c in range(ppb):
                ck, cv = copies(s_, h_, slot_, i, c)
                ck.start()
                cv.start()

    @pl.when(step == 0)
    def _():
        start_all(s, h, slot)

    nxt = step + 1

    @pl.when(nxt < pl.num_programs(0) * nh)
    def _():
        start_all(nxt // nh, nxt % nh, 1 - slot)

    slope = slope_ref[h]
    q = q_ref[...]
    qb = (q * (A_HEAD_DIM ** -0.5)).astype(BF16)
    ncol = nsel * MOBA_BLOCK
    row = lax.broadcasted_iota(jnp.int32, (t, ncol), 0)
    col = lax.broadcasted_iota(jnp.int32, (t, ncol), 1)
    blk_row = jnp.concatenate(
        [jnp.full((1, MOBA_BLOCK), block_of(i), jnp.int32) for i in range(nsel)], axis=1)
    kpos = blk_row * MOBA_BLOCK + (col & (MOBA_BLOCK - 1))
    qpos = past + row
    per_tok = MOBA_TOPK * MOBA_BLOCK
    mine = (col >= row * per_tok) & (col < (row + 1) * per_tok)
    bias = jnp.where(mine & (kpos <= qpos), -slope * (qpos - kpos).astype(F32), -jnp.inf)

    kn = kn_ref[...]
    r2 = lax.broadcasted_iota(jnp.int32, (t, t), 0)
    c2 = lax.broadcasted_iota(jnp.int32, (t, t), 1)
    s_own = lax.dot_general(qb, kn.astype(BF16), (((1,), (1,)), ((), ())), preferred_element_type=F32)
    s_own = jnp.where(c2 <= r2, s_own - slope * (r2 - c2).astype(F32), -jnp.inf)

    pltpu.make_async_copy(kbuf.at[slot], kbuf.at[slot], sem.at[slot, 0]).wait()
    pltpu.make_async_copy(vbuf.at[slot], vbuf.at[slot], sem.at[slot, 1]).wait()

    s_past = lax.dot_general(qb, kbuf[slot].astype(BF16), (((1,), (1,)), ((), ())),
                             preferred_element_type=F32) + bias
    m = jnp.maximum(jnp.max(s_past, axis=-1, keepdims=True), jnp.max(s_own, axis=-1, keepdims=True))
    p_past = jnp.exp(s_past - m)
    p_own = jnp.exp(s_own - m)
    l = jnp.sum(p_past, axis=-1, keepdims=True) + jnp.sum(p_own, axis=-1, keepdims=True)
    acc = jnp.dot(p_past.astype(BF16), vbuf[slot].astype(BF16), preferred_element_type=F32)
    acc = acc + jnp.dot(p_own.astype(BF16), vn_ref[...].astype(BF16), preferred_element_type=F32)
    o_ref[...] = acc / l


def _moba_sample(q, k_new, v_new, blocks, page_table, cache_k, cache_v, slopes, t):
    bsz, n_pages = page_table.shape
    page = cache_k.shape[1]
    past = n_pages * page
    assert past % MOBA_BLOCK == 0 and MOBA_BLOCK % page == 0 and t <= MOBA_BLOCK
    dh = A_HEAD_DIM
    rows = t * MOBA_TOPK * MOBA_BLOCK
    tok = lambda s, h, *_: (s, h)
    return pl.pallas_call(
        functools.partial(_moba_sample_body, past=past, page=page),
        grid_spec=pltpu.PrefetchScalarGridSpec(
            num_scalar_prefetch=2,
            grid=(bsz, A_HEADS),
            in_specs=[
                pl.BlockSpec(memory_space=pltpu.SMEM),
                pl.BlockSpec((t, dh), tok),
                pl.BlockSpec((t, dh), tok),
                pl.BlockSpec((t, dh), tok),
                pl.BlockSpec(memory_space=pl.ANY),
                pl.BlockSpec(memory_space=pl.ANY),
            ],
            out_specs=pl.BlockSpec((t, dh), tok),
            scratch_shapes=[
                pltpu.VMEM((2, rows, dh), F32),
                pltpu.VMEM((2, rows, dh), F32),
                pltpu.SemaphoreType.DMA((2, 2)),
            ],
        ),
        out_shape=jax.ShapeDtypeStruct((bsz * t, A_WIDTH), F32),
        compiler_params=_cparams(("arbitrary", "arbitrary")),
        name="moba_sample",
    )(blocks[:, :, :A_HEADS * MOBA_TOPK].reshape(bsz, -1), page_table, slopes, q, k_new, v_new,
      cache_k, cache_v)


def _gla_body(q_ref, k_ref, v_ref, r_ref, z_ref, w2_ref, bg_ref, go_ref, s0_ref,
              o_ref, sf_ref, st_ref, *, t_blk):
    c = pl.program_id(1)
    dk, dv = B_KEY_DIM, B_VAL_DIM

    @pl.when(c == 0)
    def _():
        for h in range(B_HEADS):
            st_ref[h] = s0_ref[0, h].T

    for h in range(B_HEADS):
        ks = slice(h * dk, (h + 1) * dk)
        vs = slice(h * dv, (h + 1) * dv)
        on, st_new = _gla_head(q_ref[:, ks], k_ref[:, ks], v_ref[:, vs], z_ref[...], r_ref[:, vs],
                               w2_ref[:, ks], bg_ref[:, ks], go_ref[...], st_ref[h], t_blk)
        st_ref[h] = st_new
        o_ref[:, vs] = on.astype(o_ref.dtype)

    @pl.when(c == pl.num_programs(1) - 1)
    def _():
        for h in range(B_HEADS):
            sf_ref[0, h] = st_ref[h].T


def _gla_head(q, k, v, z, rb, w2, bg, go, st, t_blk):
    cs = GLA_CHUNK
    dk = B_KEY_DIM

    def rows(x):
        if t_blk < cs:
            x = jnp.concatenate([x, jnp.zeros((cs - t_blk, x.shape[1]), x.dtype)], axis=0)
        return x

    q = rows(q) * (dk ** -0.5)
    k = rows(k)
    v = rows(v)
    pre = jnp.dot(rows(z), w2, precision=HIGHEST, preferred_element_type=F32) + bg
    la = (jnp.minimum(pre, 0.0) - jnp.log1p(jnp.exp(-jnp.abs(pre)))) * (1.0 / GATE_TEMP)
    ri = lax.broadcasted_iota(jnp.int32, (cs, cs), 0)
    ci = lax.broadcasted_iota(jnp.int32, (cs, cs), 1)
    if t_blk < cs:
        la = jnp.where(lax.broadcasted_iota(jnp.int32, la.shape, 0) < t_blk, la, 0.0)
    g = jnp.dot((ci <= ri).astype(F32), la, precision=HIGHEST, preferred_element_type=F32)
    g_last = g[cs - 1:cs, :]

    nsub = cs // GLA_SUB
    refs = [jnp.zeros((GLA_SUB, dk), F32)]
    for i in range(1, nsub):
        refs.append(jnp.broadcast_to(g[i * GLA_SUB - 1:i * GLA_SUB, :], (GLA_SUB, dk)))
    a = (q * jnp.exp(g - jnp.concatenate(refs, axis=0))).astype(BF16)

    far = [jnp.zeros((GLA_SUB, cs), F32)]
    for i in range(1, nsub):
        r_i = g[i * GLA_SUB - 1:i * GLA_SUB, :]
        b_i = (k * jnp.exp(jnp.minimum(r_i - g, 0.0))).astype(BF16)
        far.append(lax.dot_general(a[i * GLA_SUB:(i + 1) * GLA_SUB], b_i, (((1,), (1,)), ((), ())),
                                   preferred_element_type=F32))
    att = jnp.where((ri // GLA_SUB) > (ci // GLA_SUB), jnp.concatenate(far, axis=0), 0.0)
    for d in range(GLA_SUB):
        if d == 0:
            w = q * k
        else:
            e = jnp.exp(jnp.minimum(g - pltpu.roll(g, d, 0), 0.0))
            w = q * pltpu.roll(k, d, 0) * e
        a_d = jnp.sum(w, axis=-1, keepdims=True)
        att = jnp.where((ri - ci == d) & (ri % GLA_SUB >= d), a_d, att)

    o = lax.dot_general((q * jnp.exp(g)).astype(BF16), st.astype(BF16), (((1,), (1,)), ((), ())),
                        preferred_element_type=F32)
    vb = v.astype(BF16)
    o = o + jnp.dot(att.astype(BF16), vb, preferred_element_type=F32)
    kd = (k * jnp.exp(g_last - g)).astype(BF16)
    st_new = jnp.exp(g_last) * st + lax.dot_general(vb, kd, (((0,), (0,)), ((), ())),
                                                    preferred_element_type=F32)
    on = _rmsnorm(o[:t_blk], go) * (rb * jax.nn.sigmoid(rb))
    return on, st_new


def _gla(proj, w2p, bg, g_out, s0, t, out_dtype):
    bsz = s0.shape[0]
    t_blk = min(t, GLA_CHUNK)
    assert t % t_blk == 0
    nc = t // t_blk
    dk, dv = B_KEY_DIM, B_VAL_DIM
    assert BV_WIDTH == 2 * BK_WIDTH
    rowblk = lambda b, c: b * nc + c
    o, sf = pl.pallas_call(
        functools.partial(_gla_body, t_blk=t_blk),
        grid=(bsz, nc),
        in_specs=[
            pl.BlockSpec((t_blk, BK_WIDTH), lambda b, c: (rowblk(b, c), 0)),
            pl.BlockSpec((t_blk, BK_WIDTH), lambda b, c: (rowblk(b, c), 1)),
            pl.BlockSpec((t_blk, BV_WIDTH), lambda b, c: (rowblk(b, c), 1)),
            pl.BlockSpec((t_blk, BV_WIDTH), lambda b, c: (rowblk(b, c), 2)),
            pl.BlockSpec((t_blk, LANES), lambda b, c: (rowblk(b, c), (2 * BK_WIDTH + 2 * BV_WIDTH) // LANES)),
            pl.BlockSpec((LANES, BK_WIDTH), lambda b, c: (0, 0)),
            pl.BlockSpec((1, BK_WIDTH), lambda b, c: (0, 0)),
            pl.BlockSpec((1, dv), lambda b, c: (0, 0)),
            pl.BlockSpec((1, B_HEADS, dk, dv), lambda b, c: (b, 0, 0, 0)),
        ],
        out_specs=[
            pl.BlockSpec((t_blk, BV_WIDTH), lambda b, c: (rowblk(b, c), 0)),
            pl.BlockSpec((1, B_HEADS, dk, dv), lambda b, c: (b, 0, 0, 0)),
        ],
        out_shape=[
            jax.ShapeDtypeStruct((bsz * t, BV_WIDTH), out_dtype),
            jax.ShapeDtypeStruct(s0.shape, s0.dtype),
        ],
        scratch_shapes=[pltpu.VMEM((B_HEADS, dv, dk), F32)],
        compiler_params=_cparams(("parallel", "arbitrary")),
        name="gla",
    )(proj, proj, proj, proj, proj, w2p, bg, g_out, s0)
    return o, sf


def _layer(x, pe, s0, moba_fn, t, wts, mix_dtype):
    h1, u = _ffn(x, wts["g_ffn1"], wts["w1g"], wts["w1u"], wts["w1d"], wts["g_mix"])
    qa = _mm(u, wts["w_qa"], name="proj_qa")
    ka = _mm(u, wts["w_ka"], name="proj_ka")
    va = _mm(u, wts["w_va"], name="proj_va")
    gl = _mm(u, wts["w_gla"], name="proj_gla")
    oa = moba_fn(qa, ka, va)
    ob, s_fin = _gla(gl, wts["w_gate_b2"], wts["b_gate_b"], wts["g_gla_out"], s0, t, mix_dtype)
    mix = _merge(u, oa, ob, wts["w_ga"], wts["w_gb"], wts["w_proj_a"], wts["w_proj_b"])
    h2 = _mm(mix, wts["w_out"], residual=h1, name="out_proj")
    h3, w = _ffn(h2, wts["g_ffn2"], wts["w2g"], wts["w2u"], wts["w2d"], wts["g_ple"])
    y = _ple_final(w, h3, pe, wts["w_ple_gate"], wts["w_ple_proj"], wts["g_final"])
    return y, ka, va, s_fin


def kernel(x_prompt, x_sample, cache_k, cache_v, state_gla, page_table, p_prompt, p_sample, g_ffn1, w_ffn1_gate, w_ffn1_up, w_ffn1_down, g_mix, w_in, w_gate_b2, b_gate_b, g_gla_out, w_proj_a, w_proj_b, w_out, g_ffn2, w_ffn2_gate, w_ffn2_up, w_ffn2_down, g_ple, w_ple_gate, w_ple_proj, g_final):
    depth = w_in.shape[0]
    assert depth == 1, "single trunk layer"
    li = 0
    bp, sp, d = x_prompt.shape
    bs, ts, _ = x_sample.shape
    assert bp == 1
    bf = lambda a: a.astype(BF16)
    row = lambda a: a.reshape(1, -1)

    o_qa, o_ka, o_va = 0, A_WIDTH, 2 * A_WIDTH
    o_gla = 3 * A_WIDTH
    o_zg = o_gla + 2 * BK_WIDTH + 2 * BV_WIDTH
    o_ga = o_zg + GATE_RANK
    o_gb = o_ga + d
    wi = w_in[li]
    w_zg = jnp.pad(wi[:, o_zg:o_ga], ((0, 0), (0, LANES - GATE_RANK)))
    wts = dict(
        g_ffn1=row(g_ffn1[li]), w1g=bf(w_ffn1_gate[li]), w1u=bf(w_ffn1_up[li]), w1d=bf(w_ffn1_down[li]),
        g_mix=row(g_mix[li]),
        w_qa=bf(wi[:, o_qa:o_ka]), w_ka=bf(wi[:, o_ka:o_va]), w_va=bf(wi[:, o_va:o_gla]),
        w_gla=bf(jnp.concatenate([wi[:, o_gla:o_zg], w_zg], axis=1)),
        w_ga=bf(wi[:, o_ga:o_gb]), w_gb=bf(wi[:, o_gb:]),
        w_gate_b2=jnp.pad(w_gate_b2[li], ((0, LANES - GATE_RANK), (0, 0))),
        b_gate_b=row(b_gate_b[li]), g_gla_out=row(g_gla_out[li]),
        w_proj_a=bf(w_proj_a[li]), w_proj_b=bf(w_proj_b[li]), w_out=bf(w_out[li]),
        g_ffn2=row(g_ffn2[li]), w2g=bf(w_ffn2_gate[li]), w2u=bf(w_ffn2_up[li]), w2d=bf(w_ffn2_down[li]),
        g_ple=row(g_ple[li]), w_ple_gate=bf(w_ple_gate[li]), w_ple_proj=bf(w_ple_proj[li]),
        g_final=row(g_final),
    )
    slopes = jnp.exp2(-8.0 * jnp.arange(1, A_HEADS + 1, dtype=F32) / A_HEADS)

    ck = cache_k.reshape(cache_k.shape[1:])
    cv = cache_v.reshape(cache_v.shape[1:])

    s0_prompt = jnp.zeros((bp, B_HEADS, B_KEY_DIM, B_VAL_DIM), state_gla.dtype)
    cached_key_sums = []

    def moba_p(q, k, v):
        oa, ksum = _moba_prompt(q, k, v, slopes, ck, page_table)
        cached_key_sums.append(ksum)
        return oa

    yp, kp, vp, stp = _layer(x_prompt.reshape(sp, d), p_prompt[li].reshape(sp, -1), s0_prompt,
                             moba_p, sp, wts, BF16)

    def moba_s(q, k, v):
        blocks = _sample_select(q, k, cached_key_sums[0], ts)
        return _moba_sample(q, k, v, blocks, page_table, ck, cv, slopes, ts)

    ys, ks, vs, sts = _layer(x_sample.reshape(bs * ts, d), p_sample[li].reshape(bs * ts, -1), state_gla[li],
                             moba_s, ts, wts, F32)

    heads = (A_HEADS, A_HEAD_DIM)
    return (yp.reshape(bp, sp, d), ys.reshape(bs, ts, d),
            kp.reshape(1, bp, sp, *heads), vp.reshape(1, bp, sp, *heads),
            ks.reshape(1, bs, ts, *heads), vs.reshape(1, bs, ts, *heads),
            stp[None], sts[None])
```

```python
import functools

import jax
import jax.numpy as jnp
from jax import lax
from jax.experimental import pallas as pl
from jax.experimental.pallas import tpu as pltpu

F32 = jnp.float32
BF16 = jnp.bfloat16
HIGHEST = lax.Precision.HIGHEST

EPS = 1e-6
A_HEADS = 16
A_HEAD_DIM = 128
A_WIDTH = A_HEADS * A_HEAD_DIM
MOBA_BLOCK = 256
MOBA_TOPK = 3
B_HEADS = 4
B_KEY_DIM = 256
B_VAL_DIM = 512
BK_WIDTH = B_HEADS * B_KEY_DIM
BV_WIDTH = B_HEADS * B_VAL_DIM
GATE_RANK = 16
GATE_TEMP = 16.0
GLA_CHUNK = 64
GLA_SUB = 16

LANES = 128
VMEM_LIMIT = 52 * 1024 * 1024
PAGE_BUF_BYTES = 24 * 1024 * 1024
MASK_BIG = float(2 ** 30)

LOG2E = 1.4426950408889634
MOBA_GROUP = 4

SEL_ROWS = 32
VT_EXTRA_ROWS = 16
COL_KPOS = SEL_ROWS
COL_QPOS = COL_KPOS + 3


def _cparams(sem):
    return pltpu.CompilerParams(dimension_semantics=sem, vmem_limit_bytes=VMEM_LIMIT)


def _rmsnorm(x, g):
    return x * lax.rsqrt(jnp.mean(x * x, axis=-1, keepdims=True) + EPS) * g


def _tile(n, cap):
    if n <= cap:
        return n
    t = (cap // LANES) * LANES
    while n % t:
        t -= LANES
    return t


def _ffn_body(x_ref, g_ref, wg_ref, wu_ref, wd_ref, gn_ref, h_ref, u_ref, xn_ref):
    j = pl.program_id(1)

    @pl.when(j == 0)
    def _():
        xn_ref[...] = _rmsnorm(x_ref[...], g_ref[...]).astype(BF16)
        h_ref[...] = jnp.zeros_like(h_ref)

    xn = xn_ref[...]
    a = jnp.dot(xn, wg_ref[...], preferred_element_type=F32)
    b = jnp.dot(xn, wu_ref[...], preferred_element_type=F32)
    mid = (a * jax.nn.sigmoid(a) * b).astype(BF16)
    h_ref[...] += jnp.dot(mid, wd_ref[...], preferred_element_type=F32)

    @pl.when(j == pl.num_programs(1) - 1)
    def _():
        h = x_ref[...] + 0.5 * h_ref[...]
        h_ref[...] = h
        u_ref[...] = _rmsnorm(h, gn_ref[...]).astype(BF16)


def _ffn(x, g, wg, wu, wd, g_next):
    m, d = x.shape
    f = wg.shape[1]
    tm = _tile(m, 512)
    tf = _tile(f, 512)
    return pl.pallas_call(
        _ffn_body,
        grid=(m // tm, f // tf),
        in_specs=[
            pl.BlockSpec((tm, d), lambda i, j: (i, 0)),
            pl.BlockSpec((1, d), lambda i, j: (0, 0)),
            pl.BlockSpec((d, tf), lambda i, j: (0, j)),
            pl.BlockSpec((d, tf), lambda i, j: (0, j)),
            pl.BlockSpec((tf, d), lambda i, j: (j, 0)),
            pl.BlockSpec((1, d), lambda i, j: (0, 0)),
        ],
        out_specs=[
            pl.BlockSpec((tm, d), lambda i, j: (i, 0)),
            pl.BlockSpec((tm, d), lambda i, j: (i, 0)),
        ],
        out_shape=[jax.ShapeDtypeStruct((m, d), F32), jax.ShapeDtypeStruct((m, d), BF16)],
        scratch_shapes=[pltpu.VMEM((tm, d), BF16)],
        compiler_params=_cparams(("parallel", "arbitrary")),
        name="ffn",
    )(x, g, wg, wu, wd, g_next)


def _mm_body(x_ref, w_ref, o_ref):
    o_ref[...] = jnp.dot(x_ref[...], w_ref[...], preferred_element_type=F32).astype(o_ref.dtype)


def _mm_res_body(x_ref, w_ref, r_ref, o_ref):
    acc = jnp.dot(x_ref[...], w_ref[...], preferred_element_type=F32)
    o_ref[...] = (acc + r_ref[...]).astype(o_ref.dtype)


def _mm(x, w, out_dtype=F32, residual=None, name="mm"):
    m, k = x.shape
    n = w.shape[1]
    tm = _tile(m, 1024)
    tn = _tile(n, 1024)
    in_specs = [
        pl.BlockSpec((tm, k), lambda j, i: (i, 0)),
        pl.BlockSpec((k, tn), lambda j, i: (0, j)),
    ]
    args = [x, w]
    body = _mm_body
    if residual is not None:
        in_specs.append(pl.BlockSpec((tm, tn), lambda j, i: (i, j)))
        args.append(residual)
        body = _mm_res_body
    return pl.pallas_call(
        body,
        grid=(n // tn, m // tm),
        in_specs=in_specs,
        out_specs=pl.BlockSpec((tm, tn), lambda j, i: (i, j)),
        out_shape=jax.ShapeDtypeStruct((m, n), out_dtype),
        compiler_params=_cparams(("parallel", "parallel")),
        name=name,
    )(*args)


def _merge_body(u_ref, oa_ref, ob_ref, wga_ref, wgb_ref, wpa_ref, wpb_ref, o_ref):
    u = u_ref[...]
    ga = jnp.dot(u, wga_ref[...], preferred_element_type=F32)
    gb = jnp.dot(u, wgb_ref[...], preferred_element_type=F32)
    pa = jnp.dot(oa_ref[...].astype(BF16), wpa_ref[...], preferred_element_type=F32)
    pb = jnp.dot(ob_ref[...].astype(BF16), wpb_ref[...], preferred_element_type=F32)
    o_ref[...] = (jax.nn.sigmoid(ga) * pa + jax.nn.sigmoid(gb) * pb).astype(o_ref.dtype)


def _merge(u, oa, ob, wga, wgb, wpa, wpb):
    m, d = u.shape
    n = wga.shape[1]
    tm = _tile(m, 512)
    tn = _tile(n, 512)
    row = lambda i, j: (i, 0)
    col = lambda i, j: (0, j)
    return pl.pallas_call(
        _merge_body,
        grid=(m // tm, n // tn),
        in_specs=[
            pl.BlockSpec((tm, d), row),
            pl.BlockSpec((tm, oa.shape[1]), row),
            pl.BlockSpec((tm, ob.shape[1]), row),
            pl.BlockSpec((d, tn), col),
            pl.BlockSpec((d, tn), col),
            pl.BlockSpec((oa.shape[1], tn), col),
            pl.BlockSpec((ob.shape[1], tn), col),
        ],
        out_specs=pl.BlockSpec((tm, tn), lambda i, j: (i, j)),
        out_shape=jax.ShapeDtypeStruct((m, n), BF16),
        compiler_params=_cparams(("parallel", "parallel")),
        name="merge",
    )(u, oa, ob, wga, wgb, wpa, wpb)


def _ple_body(w_ref, h_ref, pe_ref, wpg_ref, wpp_ref, gf_ref, y_ref):
    gate = jax.nn.sigmoid(jnp.dot(w_ref[...], wpg_ref[...], preferred_element_type=F32))
    pp = jnp.dot(pe_ref[...].astype(BF16), wpp_ref[...], preferred_element_type=F32)
    y_ref[...] = _rmsnorm(h_ref[...] + gate * pp, gf_ref[...])


def _ple_final(w, h, pe, wpg, wpp, g_final):
    m, d = h.shape
    tm = _tile(m, 256)
    row = lambda i: (i, 0)
    whole = lambda i: (0, 0)
    return pl.pallas_call(
        _ple_body,
        grid=(m // tm,),
        in_specs=[
            pl.BlockSpec((tm, d), row),
            pl.BlockSpec((tm, d), row),
            pl.BlockSpec((tm, pe.shape[1]), row),
            pl.BlockSpec(wpg.shape, whole),
            pl.BlockSpec(wpp.shape, whole),
            pl.BlockSpec((1, d), whole),
        ],
        out_specs=pl.BlockSpec((tm, d), row),
        out_shape=jax.ShapeDtypeStruct((m, d), F32),
        compiler_params=_cparams(("parallel",)),
        name="ple_final",
    )(w, h, pe, wpg, wpp, g_final)


def _top3_mask(s, n_valid, axis):
    nb = s.shape[axis]
    blk = lax.broadcasted_iota(jnp.int32, s.shape, axis)
    lane_f = blk.astype(F32)
    neg = jnp.float32(-jnp.inf)
    s = jnp.where(blk < n_valid, s, neg)
    sel = jnp.zeros(s.shape, F32)
    picks = []
    for _ in range(MOBA_TOPK):
        best = jnp.max(s, axis=axis, keepdims=True)
        idx = jnp.min(jnp.where(s == best, lane_f, float(nb)), axis=axis, keepdims=True)
        hit = (lane_f == idx) & (best > neg)
        sel = jnp.where(hit, 1.0, sel)
        s = jnp.where(hit, neg, s)
        picks.append(idx)
    return sel, picks


def _split3(x):
    hi = x.astype(BF16).astype(F32)
    mid = (x - hi).astype(BF16).astype(F32)
    lo = (x - hi - mid).astype(BF16).astype(F32)
    return hi, mid, lo


def _dot_bf16x3(a, b):
    a_hi = a.astype(BF16)
    a_lo = (a - a_hi.astype(F32)).astype(BF16)
    b_hi = b.astype(BF16)
    b_lo = (b - b_hi.astype(F32)).astype(BF16)
    return jnp.dot(jnp.concatenate([a_hi, a_hi, a_lo], axis=1), jnp.concatenate([b_hi, b_lo, b_hi], axis=0),
                   preferred_element_type=F32)


def _moba_prep_body(slope_ref, q_ref, k_ref, v_ref, qx_ref, kx_ref, vt_ref, km_ref):
    j = pl.program_id(0)
    nb = pl.num_programs(0)
    blk = MOBA_BLOCK

    @pl.when(j == 0)
    def _():
        km_ref[...] = jnp.zeros_like(km_ref)

    lane = lax.broadcasted_iota(jnp.int32, (blk, LANES), 1)
    kpos = (j * blk + lax.broadcasted_iota(jnp.int32, (blk, LANES), 0)).astype(F32)
    row_sel = lax.broadcasted_iota(jnp.int32, (SEL_ROWS, blk), 0)
    row_ext = lax.broadcasted_iota(jnp.int32, (LANES - SEL_ROWS, blk), 0) + SEL_ROWS
    qpos = (j * blk + lax.broadcasted_iota(jnp.int32, (LANES - SEL_ROWS, blk), 1)).astype(F32)
    ones_row = jnp.where(lax.broadcasted_iota(jnp.int32, (VT_EXTRA_ROWS, blk), 0) == 0, 1.0, 0.0)
    for h in range(A_HEADS):
        hs = slice(h * A_HEAD_DIM, (h + 1) * A_HEAD_DIM)
        qt = q_ref[:, hs].T
        sel, _ = _top3_mask(_dot_bf16x3(km_ref[h], qt), j, axis=0)
        pen = jnp.where((row_sel == j) | (row_sel >= nb), 0.0, (sel - 1.0) * MASK_BIG)
        c_hi, c_mid, c_lo = _split3((slope_ref[h] * -LOG2E) * qpos)
        ext = jnp.where((row_ext >= COL_KPOS) & (row_ext < COL_KPOS + 3), 1.0, 0.0)
        ext = jnp.where(row_ext == COL_QPOS, c_hi, ext)
        ext = jnp.where(row_ext == COL_QPOS + 1, c_mid, ext)
        ext = jnp.where(row_ext == COL_QPOS + 2, c_lo, ext)
        qx_ref[h, 0] = jnp.concatenate([(qt * (A_HEAD_DIM ** -0.5 * LOG2E)).astype(BF16), pen.astype(BF16),
                                        ext.astype(BF16)], axis=0)
        k = k_ref[:, hs]
        km_ref[h, pl.ds(j, 1), :] = jnp.sum(k, axis=0, keepdims=True) * (1.0 / MOBA_BLOCK)
        vt_ref[h, 0] = jnp.concatenate([v_ref[:, hs].T, ones_row], axis=0).astype(BF16)
        a_hi, a_mid, a_lo = _split3((slope_ref[h] * LOG2E) * kpos)
        ext = jnp.where(lane == j, 1.0, 0.0)
        ext = jnp.where(lane == COL_KPOS, a_hi, ext)
        ext = jnp.where(lane == COL_KPOS + 1, a_mid, ext)
        ext = jnp.where(lane == COL_KPOS + 2, a_lo, ext)
        ext = jnp.where((lane >= COL_QPOS) & (lane < COL_QPOS + 3), 1.0, ext)
        kx_ref[h, 0] = jnp.concatenate([k.astype(BF16), ext.astype(BF16)], axis=1)


def _moba_prompt_body(pt_ref, qx_ref, kx_ref, vt_ref, ck_ref, o_ref, ks_ref,
                      s_buf, page_buf, sem, *, first_block, pages_per_block):
    h = pl.program_id(0)
    n_tiles = kx_ref.shape[1]
    b = first_block + pl.program_id(1)
    blk = MOBA_BLOCK
    step = h * pl.num_programs(1) + pl.program_id(1)
    slot = step % 2
    per_step = page_buf.shape[1]

    def page_copies(step_, slot_):
        return [pltpu.make_async_copy(ck_ref.at[pt_ref[step_ * per_step + c]], page_buf.at[slot_, c],
                                      sem.at[slot_]) for c in range(per_step)]

    @pl.when(step == 0)
    def _():
        for cp in page_copies(step, slot):
            cp.start()

    @pl.when(step + 1 < pl.num_programs(0) * pl.num_programs(1))
    def _():
        for cp in page_copies(step + 1, 1 - slot):
            cp.start()

    qx = qx_ref[0, 0]
    kr = lax.broadcasted_iota(jnp.int32, (blk, blk), 0)
    qc = lax.broadcasted_iota(jnp.int32, (blk, blk), 1)

    m = jnp.full((1, blk), -MASK_BIG, F32)
    for j in range(n_tiles):
        s = jnp.dot(kx_ref[0, j], qx, preferred_element_type=F32)
        if j >= first_block:
            s = jnp.where(j * blk + kr <= b * blk + qc, s, -MASK_BIG)
        s_buf[j] = s
        m = jnp.maximum(m, jnp.max(s, axis=0, keepdims=True))

    acc = jnp.zeros((A_HEAD_DIM + VT_EXTRA_ROWS, blk), F32)
    for j in range(n_tiles):
        p = jnp.exp2((s_buf[j] - m).astype(BF16))
        acc = acc + jnp.dot(vt_ref[0, j], p, preferred_element_type=F32)
    o_ref[...] = (acc[:A_HEAD_DIM] / acc[A_HEAD_DIM:A_HEAD_DIM + 1]).T.astype(o_ref.dtype)

    pltpu.make_async_copy(page_buf.at[slot], page_buf.at[slot], sem.at[slot]).wait()
    for i in range(per_step // pages_per_block):
        total = jnp.sum(page_buf[slot, i * pages_per_block], axis=0)
        for c in range(1, pages_per_block):
            total = total + jnp.sum(page_buf[slot, i * pages_per_block + c], axis=0)
        ks_ref[i] = total


def _page_shares(ncalls, total, cap):
    assert ncalls <= total <= ncalls * cap
    weights = [c + 1 for c in range(ncalls)]
    shares = [min(cap, max(1, total * w // sum(weights))) for w in weights]
    c = ncalls - 1
    while sum(shares) != total:
        step = 1 if sum(shares) < total else -1
        if 1 <= shares[c] + step <= cap:
            shares[c] += step
        c = (c - 1) % ncalls
    return shares


def _moba_prompt(q, k, v, slopes, cache_k, page_table):
    s_len = q.shape[0]
    nb = s_len // MOBA_BLOCK
    assert nb * MOBA_BLOCK == s_len and nb % MOBA_GROUP == 0 and MOBA_TOPK <= nb <= SEL_ROWS
    dh = A_HEAD_DIM
    width = A_HEADS * dh
    bsz, n_pages = page_table.shape
    page = cache_k.shape[1]
    ppb = MOBA_BLOCK // page
    ncalls = nb // MOBA_GROUP
    nsteps = A_HEADS * MOBA_GROUP
    nblk = n_pages // ppb
    assert nblk * ppb == n_pages and (bsz * nblk) % nsteps == 0
    page_bytes = 4 * page * width
    shares = _page_shares(ncalls, bsz * nblk // nsteps, PAGE_BUF_BYTES // (2 * ppb * page_bytes))
    pages_flat = page_table.reshape(-1)
    vrows = dh + VT_EXTRA_ROWS
    row_blk = pl.BlockSpec((MOBA_BLOCK, width), lambda j: (j, 0))
    qx, kx, vt = pl.pallas_call(
        _moba_prep_body,
        grid=(nb,),
        in_specs=[pl.BlockSpec(memory_space=pltpu.SMEM), row_blk, row_blk, row_blk],
        out_specs=[
            pl.BlockSpec((A_HEADS, 1, 2 * LANES, MOBA_BLOCK), lambda j: (0, j, 0, 0)),
            pl.BlockSpec((A_HEADS, 1, MOBA_BLOCK, 2 * LANES), lambda j: (0, j, 0, 0)),
            pl.BlockSpec((A_HEADS, 1, vrows, MOBA_BLOCK), lambda j: (0, j, 0, 0)),
        ],
        out_shape=[
            jax.ShapeDtypeStruct((A_HEADS, nb, 2 * LANES, MOBA_BLOCK), BF16),
            jax.ShapeDtypeStruct((A_HEADS, nb, MOBA_BLOCK, 2 * LANES), BF16),
            jax.ShapeDtypeStruct((A_HEADS, nb, vrows, MOBA_BLOCK), BF16),
        ],
        scratch_shapes=[pltpu.VMEM((A_HEADS, SEL_ROWS, dh), F32)],
        compiler_params=_cparams(("arbitrary",)),
        name="moba_prep",
    )(slopes, q, k, v)

    def attend(c):
        first = c * MOBA_GROUP
        n_tiles = first + MOBA_GROUP
        blk_step = shares[c]
        per_step = blk_step * ppb
        blk0 = nsteps * sum(shares[:c])
        return pl.pallas_call(
            functools.partial(_moba_prompt_body, first_block=first, pages_per_block=ppb),
            grid_spec=pltpu.PrefetchScalarGridSpec(
                num_scalar_prefetch=1,
                grid=(A_HEADS, MOBA_GROUP),
                in_specs=[
                    pl.BlockSpec((1, 1, 2 * LANES, MOBA_BLOCK), lambda h, i, pt: (h, first + i, 0, 0)),
                    pl.BlockSpec((1, n_tiles, MOBA_BLOCK, 2 * LANES), lambda h, i, pt: (h, 0, 0, 0)),
                    pl.BlockSpec((1, n_tiles, vrows, MOBA_BLOCK), lambda h, i, pt: (h, 0, 0, 0)),
                    pl.BlockSpec(memory_space=pl.ANY),
                ],
                out_specs=[
                    pl.BlockSpec((MOBA_BLOCK, dh), lambda h, i, pt: (i, h)),
                    pl.BlockSpec((blk_step, A_HEADS, dh), lambda h, i, pt: (h * MOBA_GROUP + i, 0, 0)),
                ],
                scratch_shapes=[
                    pltpu.VMEM((n_tiles, MOBA_BLOCK, MOBA_BLOCK), F32),
                    pltpu.VMEM((2, per_step) + cache_k.shape[1:], F32),
                    pltpu.SemaphoreType.DMA((2,)),
                ],
            ),
            out_shape=[
                jax.ShapeDtypeStruct((MOBA_GROUP * MOBA_BLOCK, A_WIDTH), BF16),
                jax.ShapeDtypeStruct((nsteps * blk_step, A_HEADS, dh), F32),
            ],
            compiler_params=_cparams(("arbitrary", "arbitrary")),
            name=f"moba_prompt_{c}",
        )(pages_flat[blk0 * ppb:(blk0 + nsteps * blk_step) * ppb], qx, kx, vt, cache_k)

    parts = [attend(c) for c in range(ncalls)]
    return (jnp.concatenate([o for o, _ in parts], axis=0),
            jnp.concatenate([ks for _, ks in parts], axis=0).reshape(bsz, nblk, A_HEADS, dh))


def _sample_select_body(q_ref, kn_ref, ks_ref, o_ref, *, n_past_blocks):
    t = q_ref.shape[0]
    nbp = ks_ref.shape[1]
    lane = lax.broadcasted_iota(jnp.int32, (t, LANES), 1)
    first = lax.broadcasted_iota(jnp.int32, (8, A_HEAD_DIM), 0) == 0
    out = jnp.zeros((t, LANES), F32)
    for h in range(A_HEADS):
        hs = slice(h * A_HEAD_DIM, (h + 1) * A_HEAD_DIM)
        own = jnp.where(first, jnp.sum(kn_ref[:, hs], axis=0, keepdims=True), 0.0)
        ksum = jnp.concatenate([ks_ref[0, :, h, :], own, jnp.zeros((LANES - nbp - 8, A_HEAD_DIM), F32)], axis=0)
        s = lax.dot_general(q_ref[:, hs], ksum * (1.0 / MOBA_BLOCK), (((1,), (1,)), ((), ())),
                            precision=HIGHEST, preferred_element_type=F32)
        _, picks = _top3_mask(s, n_past_blocks, axis=1)
        for r, idx in enumerate(picks):
            out = jnp.where(lane == h * MOBA_TOPK + r, idx, out)
    o_ref[0] = out.astype(jnp.int32)


def _sample_select(q, k_new, ksum, t):
    bsz, nbp, heads, dh = ksum.shape
    width = heads * dh
    assert nbp % 8 == 0 and nbp + 8 <= LANES and t <= 8
    return pl.pallas_call(
        functools.partial(_sample_select_body, n_past_blocks=nbp),
        grid=(bsz,),
        in_specs=[
            pl.BlockSpec((t, width), lambda s: (s, 0)),
            pl.BlockSpec((t, width), lambda s: (s, 0)),
            pl.BlockSpec((1, nbp, heads, dh), lambda s: (s, 0, 0, 0)),
        ],
        out_specs=pl.BlockSpec((1, t, LANES), lambda s: (s, 0, 0)),
        out_shape=jax.ShapeDtypeStruct((bsz, t, LANES), jnp.int32),
        compiler_params=_cparams(("parallel",)),
        name="sample_select",
    )(q, k_new, ksum)


def _moba_sample_body(blk_ref, pt_ref, slope_ref, q_ref, kn_ref, vn_ref, ck_ref, cv_ref, o_ref,
                      kbuf, vbuf, sem, *, past, page):
    s = pl.program_id(0)
    h = pl.program_id(1)
    nh = pl.num_programs(1)
    step = s * nh + h
    slot = step % 2
    t = q_ref.shape[0]
    nsel = t * MOBA_TOPK
    ppb = MOBA_BLOCK // page

    def block_at(s_, h_, i):
        return blk_ref[s_, (i // MOBA_TOPK) * (A_HEADS * MOBA_TOPK) + h_ * MOBA_TOPK + i % MOBA_TOPK]

    def block_of(i):
        return block_at(s, h, i)

    def copies(s_, h_, slot_, i, c):
        phys = pt_ref[s_, block_at(s_, h_, i) * ppb + c]
        dst = pl.ds((i * ppb + c) * page, page)
        return (pltpu.make_async_copy(ck_ref.at[phys, :, h_, :], kbuf.at[slot_, dst, :], sem.at[slot_, 0]),
                pltpu.make_async_copy(cv_ref.at[phys, :, h_, :], vbuf.at[slot_, dst, :], sem.at[slot_, 1]))

    def start_all(s_, h_, slot_):
        for i in range(nsel):
            for c in range(ppb):
                ck, cv = copies(s_, h_, slot_, i, c)
                ck.start()
                cv.start()

    @pl.when(step == 0)
    def _():
        start_all(s, h, slot)

    nxt = step + 1

    @pl.when(nxt < pl.num_programs(0) * nh)
    def _():
        start_all(nxt // nh, nxt % nh, 1 - slot)

    slope = slope_ref[h]
    q = q_ref[...]
    qb = (q * (A_HEAD_DIM ** -0.5)).astype(BF16)
    ncol = nsel * MOBA_BLOCK
    row = lax.broadcasted_iota(jnp.int32, (t, ncol), 0)
    col = lax.broadcasted_iota(jnp.int32, (t, ncol), 1)
    blk_row = jnp.concatenate(
        [jnp.full((1, MOBA_BLOCK), block_of(i), jnp.int32) for i in range(nsel)], axis=1)
    kpos = blk_row * MOBA_BLOCK + (col & (MOBA_BLOCK - 1))
    qpos = past + row
    per_tok = MOBA_TOPK * MOBA_BLOCK
    mine = (col >= row * per_tok) & (col < (row + 1) * per_tok)
    bias = jnp.where(mine & (kpos <= qpos), -slope * (qpos - kpos).astype(F32), -jnp.inf)

    kn = kn_ref[...]
    r2 = lax.broadcasted_iota(jnp.int32, (t, t), 0)
    c2 = lax.broadcasted_iota(jnp.int32, (t, t), 1)
    s_own = lax.dot_general(qb, kn.astype(BF16), (((1,), (1,)), ((), ())), preferred_element_type=F32)
    s_own = jnp.where(c2 <= r2, s_own - slope * (r2 - c2).astype(F32), -jnp.inf)

    pltpu.make_async_copy(kbuf.at[slot], kbuf.at[slot], sem.at[slot, 0]).wait()
    pltpu.make_async_copy(vbuf.at[slot], vbuf.at[slot], sem.at[slot, 1]).wait()

    s_past = lax.dot_general(qb, kbuf[slot].astype(BF16), (((1,), (1,)), ((), ())),
                             preferred_element_type=F32) + bias
    m = jnp.maximum(jnp.max(s_past, axis=-1, keepdims=True), jnp.max(s_own, axis=-1, keepdims=True))
    p_past = jnp.exp(s_past - m)
    p_own = jnp.exp(s_own - m)
    l = jnp.sum(p_past, axis=-1, keepdims=True) + jnp.sum(p_own, axis=-1, keepdims=True)
    acc = jnp.dot(p_past.astype(BF16), vbuf[slot].astype(BF16), preferred_element_type=F32)
    acc = acc + jnp.dot(p_own.astype(BF16), vn_ref[...].astype(BF16), preferred_element_type=F32)
    o_ref[...] = acc / l


def _moba_sample(q, k_new, v_new, blocks, page_table, cache_k, cache_v, slopes, t):
    bsz, n_pages = page_table.shape
    page = cache_k.shape[1]
    past = n_pages * page
    assert past % MOBA_BLOCK == 0 and MOBA_BLOCK % page == 0 and t <= MOBA_BLOCK
    dh = A_HEAD_DIM
    rows = t * MOBA_TOPK * MOBA_BLOCK
    tok = lambda s, h, *_: (s, h)
    return pl.pallas_call(
        functools.partial(_moba_sample_body, past=past, page=page),
        grid_spec=pltpu.PrefetchScalarGridSpec(
            num_scalar_prefetch=2,
            grid=(bsz, A_HEADS),
            in_specs=[
                pl.BlockSpec(memory_space=pltpu.SMEM),
                pl.BlockSpec((t, dh), tok),
                pl.BlockSpec((t, dh), tok),
                pl.BlockSpec((t, dh), tok),
                pl.BlockSpec(memory_space=pl.ANY),
                pl.BlockSpec(memory_space=pl.ANY),
            ],
            out_specs=pl.BlockSpec((t, dh), tok),
            scratch_shapes=[
                pltpu.VMEM((2, rows, dh), F32),
                pltpu.VMEM((2, rows, dh), F32),
                pltpu.SemaphoreType.DMA((2, 2)),
            ],
        ),
        out_shape=jax.ShapeDtypeStruct((bsz * t, A_WIDTH), F32),
        compiler_params=_cparams(("arbitrary", "arbitrary")),
        name="moba_sample",
    )(blocks[:, :, :A_HEADS * MOBA_TOPK].reshape(bsz, -1), page_table, slopes, q, k_new, v_new,
      cache_k, cache_v)


def _gla_body(q_ref, k_ref, v_ref, r_ref, z_ref, w2_ref, bg_ref, go_ref, s0_ref,
              o_ref, sf_ref, st_ref, *, t_blk):
    c = pl.program_id(1)
    dk, dv = B_KEY_DIM, B_VAL_DIM

    @pl.when(c == 0)
    def _():
        for h in range(B_HEADS):
            st_ref[h] = s0_ref[0, h].T

    for h in range(B_HEADS):
        ks = slice(h * dk, (h + 1) * dk)
        vs = slice(h * dv, (h + 1) * dv)
        on, st_new = _gla_head(q_ref[:, ks], k_ref[:, ks], v_ref[:, vs], z_ref[...], r_ref[:, vs],
                               w2_ref[:, ks], bg_ref[:, ks], go_ref[...], st_ref[h], t_blk)
        st_ref[h] = st_new
        o_ref[:, vs] = on.astype(o_ref.dtype)

    @pl.when(c == pl.num_programs(1) - 1)
    def _():
        for h in range(B_HEADS):
            sf_ref[0, h] = st_ref[h].T


def _gla_head(q, k, v, z, rb, w2, bg, go, st, t_blk):
    cs = min(GLA_CHUNK, -(-t_blk // GLA_SUB) * GLA_SUB)
    dk = B_KEY_DIM

    def rows(x):
        if t_blk < cs:
            x = jnp.concatenate([x, jnp.zeros((cs - t_blk, x.shape[1]), x.dtype)], axis=0)
        return x

    q = rows(q) * (dk ** -0.5)
    k = rows(k)
    v = rows(v)
    pre = jnp.dot(rows(z), w2, precision=HIGHEST, preferred_element_type=F32) + bg
    la = (jnp.minimum(pre, 0.0) - jnp.log1p(jnp.exp(-jnp.abs(pre)))) * (1.0 / GATE_TEMP)
    ri = lax.broadcasted_iota(jnp.int32, (cs, cs), 0)
    ci = lax.broadcasted_iota(jnp.int32, (cs, cs), 1)
    if t_blk < cs:
        la = jnp.where(lax.broadcasted_iota(jnp.int32, la.shape, 0) < t_blk, la, 0.0)
    g = jnp.dot((ci <= ri).astype(F32), la, precision=HIGHEST, preferred_element_type=F32)
    g_last = g[cs - 1:cs, :]

    nsub = cs // GLA_SUB
    refs = [jnp.zeros((GLA_SUB, dk), F32)]
    for i in range(1, nsub):
        refs.append(jnp.broadcast_to(g[i * GLA_SUB - 1:i * GLA_SUB, :], (GLA_SUB, dk)))
    a = (q * jnp.exp(g - jnp.concatenate(refs, axis=0))).astype(BF16)

    far = [jnp.zeros((GLA_SUB, cs), F32)]
    for i in range(1, nsub):
        r_i = g[i * GLA_SUB - 1:i * GLA_SUB, :]
        b_i = (k * jnp.exp(jnp.minimum(r_i - g, 0.0))).astype(BF16)
        far.append(lax.dot_general(a[i * GLA_SUB:(i + 1) * GLA_SUB], b_i, (((1,), (1,)), ((), ())),
                                   preferred_element_type=F32))
    att = jnp.where((ri // GLA_SUB) > (ci // GLA_SUB), jnp.concatenate(far, axis=0), 0.0)
    for d in range(GLA_SUB):
        if d == 0:
            w = q * k
        else:
            e = jnp.exp(jnp.minimum(g - pltpu.roll(g, d, 0), 0.0))
            w = q * pltpu.roll(k, d, 0) * e
        a_d = jnp.sum(w, axis=-1, keepdims=True)
        att = jnp.where((ri - ci == d) & (ri % GLA_SUB >= d), a_d, att)

    o = lax.dot_general((q * jnp.exp(g)).astype(BF16), st.astype(BF16), (((1,), (1,)), ((), ())),
                        preferred_element_type=F32)
    vb = v.astype(BF16)
    o = o + jnp.dot(att.astype(BF16), vb, preferred_element_type=F32)
    kd = (k * jnp.exp(g_last - g)).astype(BF16)
    st_new = jnp.exp(g_last) * st + lax.dot_general(vb, kd, (((0,), (0,)), ((), ())),
                                                    preferred_element_type=F32)
    on = _rmsnorm(o[:t_blk], go) * (rb * jax.nn.sigmoid(rb))
    return on, st_new


def _gla(proj, w2p, bg, g_out, s0, t, out_dtype):
    bsz = s0.shape[0]
    t_blk = min(t, GLA_CHUNK)
    assert t % t_blk == 0
    nc = t // t_blk
    dk, dv = B_KEY_DIM, B_VAL_DIM
    assert BV_WIDTH == 2 * BK_WIDTH
    rowblk = lambda b, c: b * nc + c
    o, sf = pl.pallas_call(
        functools.partial(_gla_body, t_blk=t_blk),
        grid=(bsz, nc),
        in_specs=[
            pl.BlockSpec((t_blk, BK_WIDTH), lambda b, c: (rowblk(b, c), 0)),
            pl.BlockSpec((t_blk, BK_WIDTH), lambda b, c: (rowblk(b, c), 1)),
            pl.BlockSpec((t_blk, BV_WIDTH), lambda b, c: (rowblk(b, c), 1)),
            pl.BlockSpec((t_blk, BV_WIDTH), lambda b, c: (rowblk(b, c), 2)),
            pl.BlockSpec((t_blk, LANES), lambda b, c: (rowblk(b, c), (2 * BK_WIDTH + 2 * BV_WIDTH) // LANES)),
            pl.BlockSpec((LANES, BK_WIDTH), lambda b, c: (0, 0)),
            pl.BlockSpec((1, BK_WIDTH), lambda b, c: (0, 0)),
            pl.BlockSpec((1, dv), lambda b, c: (0, 0)),
            pl.BlockSpec((1, B_HEADS, dk, dv), lambda b, c: (b, 0, 0, 0)),
        ],
        out_specs=[
            pl.BlockSpec((t_blk, BV_WIDTH), lambda b, c: (rowblk(b, c), 0)),
            pl.BlockSpec((1, B_HEADS, dk, dv), lambda b, c: (b, 0, 0, 0)),
        ],
        out_shape=[
            jax.ShapeDtypeStruct((bsz * t, BV_WIDTH), out_dtype),
            jax.ShapeDtypeStruct(s0.shape, s0.dtype),
        ],
        scratch_shapes=[pltpu.VMEM((B_HEADS, dv, dk), F32)],
        compiler_params=_cparams(("parallel", "arbitrary")),
        name="gla",
    )(proj, proj, proj, proj, proj, w2p, bg, g_out, s0)
    return o, sf


def _layer(x, pe, s0, moba_fn, t, wts, mix_dtype):
    h1, u = _ffn(x, wts["g_ffn1"], wts["w1g"], wts["w1u"], wts["w1d"], wts["g_mix"])
    qa = _mm(u, wts["w_qa"], name="proj_qa")
    ka = _mm(u, wts["w_ka"], name="proj_ka")
    va = _mm(u, wts["w_va"], name="proj_va")
    gl = _mm(u, wts["w_gla"], name="proj_gla")
    oa = moba_fn(qa, ka, va)
    ob, s_fin = _gla(gl, wts["w_gate_b2"], wts["b_gate_b"], wts["g_gla_out"], s0, t, mix_dtype)
    mix = _merge(u, oa, ob, wts["w_ga"], wts["w_gb"], wts["w_proj_a"], wts["w_proj_b"])
    h2 = _mm(mix, wts["w_out"], residual=h1, name="out_proj")
    h3, w = _ffn(h2, wts["g_ffn2"], wts["w2g"], wts["w2u"], wts["w2d"], wts["g_ple"])
    y = _ple_final(w, h3, pe, wts["w_ple_gate"], wts["w_ple_proj"], wts["g_final"])
    return y, ka, va, s_fin


def kernel(x_prompt, x_sample, cache_k, cache_v, state_gla, page_table, p_prompt, p_sample, g_ffn1, w_ffn1_gate, w_ffn1_up, w_ffn1_down, g_mix, w_in, w_gate_b2, b_gate_b, g_gla_out, w_proj_a, w_proj_b, w_out, g_ffn2, w_ffn2_gate, w_ffn2_up, w_ffn2_down, g_ple, w_ple_gate, w_ple_proj, g_final):
    depth = w_in.shape[0]
    assert depth == 1, "single trunk layer"
    li = 0
    bp, sp, d = x_prompt.shape
    bs, ts, _ = x_sample.shape
    assert bp == 1
    bf = lambda a: a.astype(BF16)
    row = lambda a: a.reshape(1, -1)

    o_qa, o_ka, o_va = 0, A_WIDTH, 2 * A_WIDTH
    o_gla = 3 * A_WIDTH
    o_zg = o_gla + 2 * BK_WIDTH + 2 * BV_WIDTH
    o_ga = o_zg + GATE_RANK
    o_gb = o_ga + d
    wi = w_in[li]
    w_zg = jnp.pad(wi[:, o_zg:o_ga], ((0, 0), (0, LANES - GATE_RANK)))
    wts = dict(
        g_ffn1=row(g_ffn1[li]), w1g=bf(w_ffn1_gate[li]), w1u=bf(w_ffn1_up[li]), w1d=bf(w_ffn1_down[li]),
        g_mix=row(g_mix[li]),
        w_qa=bf(wi[:, o_qa:o_ka]), w_ka=bf(wi[:, o_ka:o_va]), w_va=bf(wi[:, o_va:o_gla]),
        w_gla=bf(jnp.concatenate([wi[:, o_gla:o_zg], w_zg], axis=1)),
        w_ga=bf(wi[:, o_ga:o_gb]), w_gb=bf(wi[:, o_gb:]),
        w_gate_b2=jnp.pad(w_gate_b2[li], ((0, LANES - GATE_RANK), (0, 0))),
        b_gate_b=row(b_gate_b[li]), g_gla_out=row(g_gla_out[li]),
        w_proj_a=bf(w_proj_a[li]), w_proj_b=bf(w_proj_b[li]), w_out=bf(w_out[li]),
        g_ffn2=row(g_ffn2[li]), w2g=bf(w_ffn2_gate[li]), w2u=bf(w_ffn2_up[li]), w2d=bf(w_ffn2_down[li]),
        g_ple=row(g_ple[li]), w_ple_gate=bf(w_ple_gate[li]), w_ple_proj=bf(w_ple_proj[li]),
        g_final=row(g_final),
    )
    slopes = jnp.exp2(-8.0 * jnp.arange(1, A_HEADS + 1, dtype=F32) / A_HEADS)

    ck = cache_k.reshape(cache_k.shape[1:])
    cv = cache_v.reshape(cache_v.shape[1:])

    s0_prompt = jnp.zeros((bp, B_HEADS, B_KEY_DIM, B_VAL_DIM), state_gla.dtype)
    cached_key_sums = []

    def moba_p(q, k, v):
        oa, ksum = _moba_prompt(q, k, v, slopes, ck, page_table)
        cached_key_sums.append(ksum)
        return oa

    yp, kp, vp, stp = _layer(x_prompt.reshape(sp, d), p_prompt[li].reshape(sp, -1), s0_prompt,
                             moba_p, sp, wts, BF16)

    def moba_s(q, k, v):
        blocks = _sample_select(q, k, cached_key_sums[0], ts)
        return _moba_sample(q, k, v, blocks, page_table, ck, cv, slopes, ts)

    ys, ks, vs, sts = _layer(x_sample.reshape(bs * ts, d), p_sample[li].reshape(bs * ts, -1), state_gla[li],
                             moba_s, ts, wts, F32)

    heads = (A_HEADS, A_HEAD_DIM)
    return (yp.reshape(bp, sp, d), ys.reshape(bs, ts, d),
            kp.reshape(1, bp, sp, *heads), vp.reshape(1, bp, sp, *heads),
            ks.reshape(1, bs, ts, *heads), vs.reshape(1, bs, ts, *heads),
            stp[None], sts[None])
```
